```python
import jax, jax.numpy as jnp
from jax import lax
import numpy as np

D_MODEL = 1024
BATCH = 4
SEQ = 4096
DEPTH = 1
DEC_BATCH = 32
DEC_SEQ = 8
PAST_LEN = 8192
PAGE_SIZE = 128

HEAD_DIM = 64
A_HEADS = 8
A_KV_HEADS = 2
IDX_HEADS = 4
IDX_DIM = 64
IDX_TOPK_MAX = 256
B_HEADS = 8
B_KV_HEADS = 2
NSA_BRANCHES = 3
CMP_LEN = 32
CMP_STRIDE = 16
SEL_BLOCK = 64
SEL_TOPN = 16
WINDOW = 512
N_EXPERTS = 32
TOP_K = 4
D_FF = 1024
SWIGLU_LIMIT = 7.0
SWIGLU_ALPHA = 1.702
MOE_BLOCK = 128
Q_BLOCK = 128
ROPE_THETA = 10000.0
RMS_EPS = 1e-5
PROJ_WIDTHS = (A_HEADS * HEAD_DIM, A_KV_HEADS * HEAD_DIM, A_KV_HEADS * HEAD_DIM, IDX_HEADS * IDX_DIM, IDX_DIM, IDX_HEADS, B_HEADS * HEAD_DIM, B_KV_HEADS * HEAD_DIM, B_KV_HEADS * HEAD_DIM, B_KV_HEADS * HEAD_DIM, B_KV_HEADS * HEAD_DIM, B_KV_HEADS * HEAD_DIM, B_KV_HEADS * HEAD_DIM, B_HEADS * NSA_BRANCHES, D_MODEL, D_MODEL)
IN_WIDTH = sum(PROJ_WIDTHS)

kernel_name = 'hybrid_dsa_nsa_moe_step'


def rmsnorm(x, g):
    xf = x.astype(jnp.float32)
    y = xf * lax.rsqrt(jnp.mean(xf * xf, axis=-1, keepdims=True) + RMS_EPS)
    return (y * g.astype(jnp.float32)).astype(x.dtype)


def rope(x, pos):
    half = x.shape[-1] // 2
    inv = jnp.power(ROPE_THETA, -jnp.arange(half, dtype=jnp.float32) / half)
    ang = pos.astype(jnp.float32)[:, None] * inv[None, :]
    cos = jnp.cos(ang)[None, :, None, :]
    sin = jnp.sin(ang)[None, :, None, :]
    xf = x.astype(jnp.float32)
    x1, x2 = xf[..., :half], xf[..., half:]
    return jnp.concatenate([x1 * cos - x2 * sin, x2 * cos + x1 * sin], axis=-1).astype(x.dtype)


def masked_softmax(logits, mask):
    l = jnp.where(mask, logits.astype(jnp.float32), -jnp.inf)
    m = jnp.max(l, axis=-1, keepdims=True)
    m = jnp.where(jnp.isfinite(m), m, 0.0)
    e = jnp.where(mask, jnp.exp(l - m), 0.0)
    s = jnp.sum(e, axis=-1, keepdims=True)
    return e / jnp.where(s > 0, s, 1.0)


def pad_seq(a, length):
    widths = [(0, 0)] * a.ndim
    widths[1] = (0, length - a.shape[1])
    return jnp.pad(a, widths)


def gather_pages(cache, page_table):
    pages = cache[page_table]
    return pages.reshape(page_table.shape[0], page_table.shape[1] * cache.shape[1], *cache.shape[2:])


def compress(rows, w_c):
    B, L = rows.shape[:2]
    n_c = (L - CMP_LEN) // CMP_STRIDE + 1
    r = rows[:, :(n_c + 1) * CMP_STRIDE].reshape(B, n_c + 1, CMP_STRIDE, *rows.shape[2:])
    first = jnp.einsum('bnjhd,jde->bnhe', r[:, :-1], w_c[:CMP_STRIDE])
    second = jnp.einsum('bnjhd,jde->bnhe', r[:, 1:], w_c[CMP_STRIDE:])
    return first + second


def dsa_block(q, qi, wi, k, v, ki, q_pos, k_top):
    B, Tq = q.shape[:2]
    L = k.shape[1]
    rel = jax.nn.relu(jnp.einsum('bqhd,bld->bqhl', qi, ki).astype(jnp.float32) * IDX_DIM ** -0.5)
    score = jnp.einsum('bqhl,bqh->bql', rel, wi.astype(jnp.float32)) * IDX_HEADS ** -0.5
    causal = jnp.arange(L)[None, None, :] <= q_pos[None, :, None]
    score = jnp.where(causal, score, -jnp.inf)
    top_val, top_idx = lax.top_k(score, k_top)
    keep = jnp.isfinite(top_val)
    bidx = jnp.arange(B)[:, None, None]
    kg = k[bidx, top_idx]
    vg = v[bidx, top_idx]
    qg = q.reshape(B, Tq, A_KV_HEADS, A_HEADS // A_KV_HEADS, HEAD_DIM)
    logits = jnp.einsum('bqjgd,bqnjd->bqjgn', qg, kg) * HEAD_DIM ** -0.5
    p = masked_softmax(logits, keep[:, :, None, None, :])
    o = jnp.einsum('bqjgn,bqnjd->bqjgd', p.astype(vg.dtype), vg)
    return o.reshape(B, Tq, A_HEADS * HEAD_DIM)


def nsa_block(q, gate, kc, vc, ks, vs, kw, vw, kw_pos, q_pos):
    B, Tq = q.shape[:2]
    G = B_HEADS // B_KV_HEADS
    scale = HEAD_DIM ** -0.5
    qg = q.reshape(B, Tq, B_KV_HEADS, G, HEAD_DIM)
    n_c = kc.shape[1]
    c_start = jnp.arange(n_c) * CMP_STRIDE
    c_mask = (c_start + CMP_LEN - 1)[None, :] <= q_pos[:, None]
    lc = jnp.einsum('bqjgd,bnjd->bqjgn', qg, kc) * scale
    pc = masked_softmax(lc, c_mask[None, :, None, None, :])
    o_cmp = jnp.einsum('bqjgn,bnjd->bqjgd', pc.astype(vc.dtype), vc).astype(jnp.float32)
    n_s = ks.shape[1] // SEL_BLOCK
    blk = jnp.arange(n_s)
    s_start = blk * SEL_BLOCK
    overlap = ((c_start[:, None] < s_start[None, :] + SEL_BLOCK) & (c_start[:, None] + CMP_LEN > s_start[None, :])).astype(jnp.float32)
    imp = jnp.einsum('bqjgn,ns->bqjs', pc, overlap)
    forced = (blk[None, :] == 0) | (blk[None, :] == (q_pos // SEL_BLOCK)[:, None])
    visible = s_start[None, :] <= q_pos[:, None]
    imp = jnp.where(forced[None, :, None, :], jnp.inf, jnp.where(visible[None, :, None, :], imp, -jnp.inf))
    n_sel = min(SEL_TOPN, n_s)
    sel_val, sel_idx = lax.top_k(imp, n_sel)
    tok = sel_idx[..., None] * SEL_BLOCK + jnp.arange(SEL_BLOCK)
    s_mask = (sel_val > -jnp.inf)[..., None] & (tok <= q_pos[None, :, None, None, None])
    tok = tok.reshape(B, Tq, B_KV_HEADS, n_sel * SEL_BLOCK)
    s_mask = s_mask.reshape(B, Tq, B_KV_HEADS, n_sel * SEL_BLOCK)
    bidx = jnp.arange(B)[:, None, None, None]
    hidx = jnp.arange(B_KV_HEADS)[None, None, :, None]
    ksg = ks[bidx, tok, hidx]
    vsg = vs[bidx, tok, hidx]
    ls = jnp.einsum('bqjgd,bqjnd->bqjgn', qg, ksg) * scale
    ps = masked_softmax(ls, s_mask[:, :, :, None, :])
    o_slc = jnp.einsum('bqjgn,bqjnd->bqjgd', ps.astype(vsg.dtype), vsg).astype(jnp.float32)
    dist = q_pos[:, None] - kw_pos[None, :]
    w_mask = (kw_pos[None, :] >= 0) & (dist >= 0) & (dist < WINDOW)
    lw = jnp.einsum('bqjgd,bkjd->bqjgk', qg, kw) * scale
    pw = masked_softmax(lw, w_mask[None, :, None, None, :])
    o_win = jnp.einsum('bqjgk,bkjd->bqjgd', pw.astype(vw.dtype), vw).astype(jnp.float32)
    g = jax.nn.sigmoid(gate.astype(jnp.float32)).reshape(B, Tq, B_KV_HEADS, G, NSA_BRANCHES)
    o = g[..., 0:1] * o_cmp + g[..., 1:2] * o_slc + g[..., 2:3] * o_win
    return o.reshape(B, Tq, B_HEADS * HEAD_DIM).astype(q.dtype)


def mix_inputs(x, pos, g_mix, w_in):
    B, T, _ = x.shape
    h = rmsnorm(x, g_mix)
    z = h @ w_in
    (qa, ka, va, qi, ki, wi, qb, kcb, vcb, ksb, vsb, kwb, vwb, gb, gate_a, gate_b) = jnp.split(z, np.cumsum(PROJ_WIDTHS)[:-1].tolist(), axis=-1)
    heads = lambda a, n: a.reshape(B, T, n, -1)
    qa = rope(heads(qa, A_HEADS), pos)
    ka = rope(heads(ka, A_KV_HEADS), pos)
    va = heads(va, A_KV_HEADS)
    qi = rope(heads(qi, IDX_HEADS), pos)
    ki = rope(ki[:, :, None, :], pos)[:, :, 0]
    qb = rope(heads(qb, B_HEADS), pos)
    kcb = rope(heads(kcb, B_KV_HEADS), pos)
    vcb = heads(vcb, B_KV_HEADS)
    ksb = rope(heads(ksb, B_KV_HEADS), pos)
    vsb = heads(vsb, B_KV_HEADS)
    kwb = rope(heads(kwb, B_KV_HEADS), pos)
    vwb = heads(vwb, B_KV_HEADS)
    gb = gb.reshape(B, T, B_HEADS, NSA_BRANCHES)
    parts = (qa, ka, va, qi, ki, wi, qb, kcb, vcb, ksb, vsb, kwb, vwb, gb)
    return parts, gate_a, gate_b


def prompt_mixers(parts, w_ck, w_cv):
    qa, ka, va, qi, ki, wi, qb, kcb, vcb, ksb, vsb, kwb, vwb, gb = parts
    B, S = qa.shape[:2]
    bq = min(Q_BLOCK, S)
    n_blk = S // bq
    k_top = min(IDX_TOPK_MAX, S // 4)
    kc = compress(kcb, w_ck)
    vc = compress(vcb, w_cv)
    n_s = -(-S // SEL_BLOCK)
    ks_p = pad_seq(ksb, n_s * SEL_BLOCK)
    vs_p = pad_seq(vsb, n_s * SEL_BLOCK)
    left = [(0, 0), (WINDOW, 0), (0, 0), (0, 0)]
    kw_p = jnp.pad(kwb, left)
    vw_p = jnp.pad(vwb, left)

    def body(i):
        q0 = i * bq
        q_pos = q0 + jnp.arange(bq)
        take = lambda a: lax.dynamic_slice_in_dim(a, q0, bq, axis=1)
        oa = dsa_block(take(qa), take(qi), take(wi), ka, va, ki, q_pos, k_top)
        kw_blk = lax.dynamic_slice_in_dim(kw_p, q0, WINDOW + bq, axis=1)
        vw_blk = lax.dynamic_slice_in_dim(vw_p, q0, WINDOW + bq, axis=1)
        kw_pos = q0 - WINDOW + jnp.arange(WINDOW + bq)
        ob = nsa_block(take(qb), take(gb), kc, vc, ks_p, vs_p, kw_blk, vw_blk, kw_pos, q_pos)
        return oa, ob

    oa, ob = lax.map(body, jnp.arange(n_blk))
    unblock = lambda o: jnp.moveaxis(o, 0, 1).reshape(B, S, -1)
    w_keep = min(WINDOW, S)
    state = (ka, va, ki, kcb, vcb, ksb, vsb, kwb[:, S - w_keep:], vwb[:, S - w_keep:])
    return unblock(oa), unblock(ob), state


def sample_mixers(parts, caches, page_table, w_ck, w_cv):
    qa, ka, va, qi, ki, wi, qb, kcb, vcb, ksb, vsb, kwb, vwb, gb = parts
    c_ak, c_av, c_aik, c_bck, c_bcv, c_bsk, c_bsv, s_wk, s_wv = caches
    T = qa.shape[1]
    past = page_table.shape[1] * PAGE_SIZE
    L = past + T
    full = lambda c, new: jnp.concatenate([gather_pages(c, page_table), new], axis=1)
    q_pos = past + jnp.arange(T)
    k_top = min(IDX_TOPK_MAX, L // 4)
    oa = dsa_block(qa, qi, wi, full(c_ak, ka), full(c_av, va), full(c_aik, ki), q_pos, k_top)
    kc = compress(full(c_bck, kcb), w_ck)
    vc = compress(full(c_bcv, vcb), w_cv)
    n_s = -(-L // SEL_BLOCK)
    ks_p = pad_seq(full(c_bsk, ksb), n_s * SEL_BLOCK)
    vs_p = pad_seq(full(c_bsv, vsb), n_s * SEL_BLOCK)
    w_buf = s_wk.shape[1]
    kw_all = jnp.concatenate([s_wk, kwb], axis=1)
    vw_all = jnp.concatenate([s_wv, vwb], axis=1)
    kw_pos = past - w_buf + jnp.arange(w_buf + T)
    ob = nsa_block(qb, gb, kc, vc, ks_p, vs_p, kw_all, vw_all, kw_pos, q_pos)
    state = (ka, va, ki, kcb, vcb, ksb, vsb, kw_all[:, T:], vw_all[:, T:])
    return oa, ob, state


def moe(h, w_router, b_router, w_ug, b_ug, w_down, b_down):
    B, T, D = h.shape
    n_tok = B * T
    hf = h.reshape(n_tok, D)
    logits = (hf @ w_router).astype(jnp.float32) + b_router.astype(jnp.float32)
    top_logit, top_e = lax.top_k(logits, TOP_K)
    weight = jax.nn.softmax(top_logit, axis=-1)
    n_asg = n_tok * TOP_K
    e_flat = top_e.reshape(n_asg)
    tok_flat = jnp.repeat(jnp.arange(n_tok), TOP_K)
    w_flat = weight.reshape(n_asg)
    order = jnp.argsort(e_flat)
    e_sorted = e_flat[order]
    counts = jnp.bincount(e_flat, length=N_EXPERTS)
    padded = (counts + MOE_BLOCK - 1) // MOE_BLOCK * MOE_BLOCK
    start = jnp.cumsum(counts) - counts
    pad_end = jnp.cumsum(padded)
    dest = (pad_end - padded)[e_sorted] + jnp.arange(n_asg) - start[e_sorted]
    n_blocks = -(-(n_asg + N_EXPERTS * (MOE_BLOCK - 1)) // MOE_BLOCK)
    n_rows = n_blocks * MOE_BLOCK
    row_tok = jnp.zeros((n_rows,), jnp.int32).at[dest].set(tok_flat[order])
    row_w = jnp.zeros((n_rows,), jnp.float32).at[dest].set(w_flat[order])
    blk_e = jnp.minimum(jnp.searchsorted(pad_end, jnp.arange(n_blocks) * MOE_BLOCK, side='right'), N_EXPERTS - 1)

    def expert_block(args):
        tok, wt, e = args
        xb = hf[tok]
        ug = xb @ w_ug[e] + b_ug[e]
        gt = jnp.minimum(ug[:, :D_FF], SWIGLU_LIMIT)
        up = jnp.clip(ug[:, D_FF:], -SWIGLU_LIMIT, SWIGLU_LIMIT)
        act = (up + 1.0) * gt * jax.nn.sigmoid(gt * SWIGLU_ALPHA)
        out = act @ w_down[e] + b_down[e]
        return out * wt[:, None].astype(out.dtype)

    outs = lax.map(expert_block, (row_tok.reshape(n_blocks, MOE_BLOCK), row_w.reshape(n_blocks, MOE_BLOCK), blk_e))
    y = jnp.zeros((n_tok, D), h.dtype).at[row_tok].add(outs.reshape(n_rows, D).astype(h.dtype))
    return y.reshape(B, T, D)


def finish(x, oa, ob, gate_a, gate_b, w_proj_a, w_proj_b, w_out, g_ffn, w_router, b_router, w_ug, b_ug, w_down, b_down):
    merged = jax.nn.sigmoid(gate_a) * (oa @ w_proj_a) + jax.nn.sigmoid(gate_b) * (ob @ w_proj_b)
    x = x + merged @ w_out
    return x + moe(rmsnorm(x, g_ffn), w_router, b_router, w_ug, b_ug, w_down, b_down)


def setup_inputs(seed: int = 0) -> dict:
    key = jax.random.key(seed)
    k = jax.random.split(key, 27)
    n_pages = PAST_LEN // PAGE_SIZE
    n_pool = DEC_BATCH * n_pages * 5 // 4
    w_buf = min(WINDOW, PAST_LEN)

    def nrm(kk, shape, scale=1.0):
        return scale * jax.random.normal(kk, shape, jnp.float32)

    a_pages = (DEPTH, n_pool, PAGE_SIZE, A_KV_HEADS, HEAD_DIM)
    b_pages = (DEPTH, n_pool, PAGE_SIZE, B_KV_HEADS, HEAD_DIM)
    win = (DEPTH, DEC_BATCH, w_buf, B_KV_HEADS, HEAD_DIM)
    page_table = jax.random.permutation(k[11], n_pool)[:DEC_BATCH * n_pages].reshape(DEC_BATCH, n_pages).astype(jnp.int32)
    return {
        'x_prompt': nrm(k[0], (BATCH, SEQ, D_MODEL)),
        'x_sample': nrm(k[1], (DEC_BATCH, DEC_SEQ, D_MODEL)),
        'cache_a_k': nrm(k[2], a_pages),
        'cache_a_v': nrm(k[3], a_pages),
        'cache_a_idx_k': nrm(k[4], (DEPTH, n_pool, PAGE_SIZE, IDX_DIM)),
        'cache_b_cmp_k': nrm(k[5], b_pages),
        'cache_b_cmp_v': nrm(k[6], b_pages),
        'cache_b_slc_k': nrm(k[7], b_pages),
        'cache_b_slc_v': nrm(k[8], b_pages),
        'state_b_win_k': nrm(k[9], win),
        'state_b_win_v': nrm(k[10], win),
        'page_table': page_table,
        'g_mix': 1.0 + nrm(k[12], (DEPTH, D_MODEL), 0.1),
        'w_in': nrm(k[13], (DEPTH, D_MODEL, IN_WIDTH), D_MODEL ** -0.5),
        'w_cmp_k': nrm(k[14], (DEPTH, CMP_LEN, HEAD_DIM, HEAD_DIM), (CMP_LEN * HEAD_DIM) ** -0.5),
        'w_cmp_v': nrm(k[15], (DEPTH, CMP_LEN, HEAD_DIM, HEAD_DIM), (CMP_LEN * HEAD_DIM) ** -0.5),
        'w_proj_a': nrm(k[16], (DEPTH, A_HEADS * HEAD_DIM, D_MODEL), (A_HEADS * HEAD_DIM) ** -0.5),
        'w_proj_b': nrm(k[17], (DEPTH, B_HEADS * HEAD_DIM, D_MODEL), (B_HEADS * HEAD_DIM) ** -0.5),
        'w_out': nrm(k[18], (DEPTH, D_MODEL, D_MODEL), D_MODEL ** -0.5),
        'g_ffn': 1.0 + nrm(k[19], (DEPTH, D_MODEL), 0.1),
        'w_router': nrm(k[20], (DEPTH, D_MODEL, N_EXPERTS), D_MODEL ** -0.5),
        'b_router': nrm(k[21], (DEPTH, N_EXPERTS), 0.01),
        'w_ug': nrm(k[22], (DEPTH, N_EXPERTS, D_MODEL, 2 * D_FF), D_MODEL ** -0.5),
        'b_ug': nrm(k[23], (DEPTH, N_EXPERTS, 2 * D_FF), 0.01),
        'w_down': nrm(k[24], (DEPTH, N_EXPERTS, D_FF, D_MODEL), D_FF ** -0.5),
        'b_down': nrm(k[25], (DEPTH, N_EXPERTS, D_MODEL), 0.01),
        'g_final': 1.0 + nrm(k[26], (D_MODEL,), 0.1),
    }


def reference(x_prompt, x_sample, cache_a_k, cache_a_v, cache_a_idx_k, cache_b_cmp_k, cache_b_cmp_v, cache_b_slc_k, cache_b_slc_v, state_b_win_k, state_b_win_v, page_table, g_mix, w_in, w_cmp_k, w_cmp_v, w_proj_a, w_proj_b, w_out, g_ffn, w_router, b_router, w_ug, b_ug, w_down, b_down, g_final):
    pos_p = jnp.arange(x_prompt.shape[1])
    pos_s = page_table.shape[1] * PAGE_SIZE + jnp.arange(x_sample.shape[1])
    new_p = [[] for _ in range(9)]
    new_s = [[] for _ in range(9)]
    x_p, x_s = x_prompt, x_sample
    for l in range(DEPTH):
        ffn_l = (g_ffn[l], w_router[l], b_router[l], w_ug[l], b_ug[l], w_down[l], b_down[l])
        parts_p, ga_p, gb_p = mix_inputs(x_p, pos_p, g_mix[l], w_in[l])
        oa_p, ob_p, st_p = prompt_mixers(parts_p, w_cmp_k[l], w_cmp_v[l])
        x_p = finish(x_p, oa_p, ob_p, ga_p, gb_p, w_proj_a[l], w_proj_b[l], w_out[l], *ffn_l)
        caches_l = (cache_a_k[l], cache_a_v[l], cache_a_idx_k[l], cache_b_cmp_k[l], cache_b_cmp_v[l], cache_b_slc_k[l], cache_b_slc_v[l], state_b_win_k[l], state_b_win_v[l])
        parts_s, ga_s, gb_s = mix_inputs(x_s, pos_s, g_mix[l], w_in[l])
        oa_s, ob_s, st_s = sample_mixers(parts_s, caches_l, page_table, w_cmp_k[l], w_cmp_v[l])
        x_s = finish(x_s, oa_s, ob_s, ga_s, gb_s, w_proj_a[l], w_proj_b[l], w_out[l], *ffn_l)
        for acc, s in zip(new_p, st_p):
            acc.append(s)
        for acc, s in zip(new_s, st_s):
            acc.append(s)
    sp = [jnp.stack(a) for a in new_p]
    ss = [jnp.stack(a) for a in new_s]
    y_prompt = rmsnorm(x_p, g_final)
    y_sample = rmsnorm(x_s, g_final)
    return (y_prompt, y_sample, sp[0], sp[1], sp[2], sp[3], sp[4], sp[5], sp[6], sp[7], sp[8], ss[0], ss[1], ss[2], ss[3], ss[4], ss[5], ss[6], ss[7], ss[8])
```

```python
import functools

import numpy as np
import jax
import jax.numpy as jnp
from jax import lax
from jax.experimental import pallas as pl
from jax.experimental.pallas import tpu as pltpu

D_MODEL = 1024
HEAD_DIM = 64
PAGE_SIZE = 128
A_HEADS = 8
A_KV_HEADS = 2
IDX_HEADS = 4
IDX_DIM = 64
IDX_TOPK_MAX = 256
B_HEADS = 8
B_KV_HEADS = 2
NSA_BRANCHES = 3
CMP_LEN = 32
CMP_STRIDE = 16
SEL_BLOCK = 64
SEL_TOPN = 16
WINDOW = 512
N_EXPERTS = 32
TOP_K = 4
D_FF = 1024
SWIGLU_LIMIT = 7.0
SWIGLU_ALPHA = 1.702
ROPE_THETA = 10000.0
RMS_EPS = 1e-5
PROJ_WIDTHS = (A_HEADS * HEAD_DIM, A_KV_HEADS * HEAD_DIM, A_KV_HEADS * HEAD_DIM, IDX_HEADS * IDX_DIM, IDX_DIM,
               IDX_HEADS, B_HEADS * HEAD_DIM, B_KV_HEADS * HEAD_DIM, B_KV_HEADS * HEAD_DIM, B_KV_HEADS * HEAD_DIM,
               B_KV_HEADS * HEAD_DIM, B_KV_HEADS * HEAD_DIM, B_KV_HEADS * HEAD_DIM, B_HEADS * NSA_BRANCHES,
               D_MODEL, D_MODEL)

LANES = 128
GQA = A_HEADS // A_KV_HEADS
KV_W = A_KV_HEADS * HEAD_DIM
VMEM_LIMIT = 48 * 1024 * 1024

ZB = dict(qa=(0, 512), qb=(512, 512), qi=(1024, 256), ka=(1280, 128), va=(1408, 128), misc=(1536, 128),
          kcb=(1664, 128), vcb=(1792, 128), ksb=(1920, 128), vsb=(2048, 128), kwb=(2176, 128), vwb=(2304, 128))
ZB_W = 2432
ZF_GATES = 2 * D_MODEL
ZF_W = ZF_GATES + ZB_W
MISC_WI = IDX_DIM
MISC_GB = IDX_DIM + IDX_HEADS
ROPE_GROUPS = ("qa", "qb", "qi", "ka", "kcb", "ksb", "kwb")

NEG = -1e30
INT_MIN = -2 ** 31
KEY_PINF = 0x7F800000
KEY_NINF = -0x7F800001

_NT = (((1,), (1,)), ((), ()))


def _cparams(*sem):
    return pltpu.CompilerParams(dimension_semantics=sem, vmem_limit_bytes=VMEM_LIMIT)


def _inproj_kernel(x_ref, g_ref, w_ref, cs_ref, zf_ref, zb_ref):
    x = x_ref[...]
    ms = jnp.mean(x * x, axis=-1, keepdims=True)
    h = (x * lax.rsqrt(ms + RMS_EPS) * g_ref[...]).astype(jnp.bfloat16)
    tm = x.shape[0]
    lane = lax.broadcasted_iota(jnp.int32, (tm, LANES), 1)
    first_half = (lane % HEAD_DIM) < (HEAD_DIM // 2)

    def rope(z, c, s):
        partner = jnp.where(first_half, pltpu.roll(z, LANES - HEAD_DIM // 2, 1), pltpu.roll(z, HEAD_DIM // 2, 1))
        return z * c + partner * s

    cos, sin = cs_ref[:, 0:LANES], cs_ref[:, LANES:2 * LANES]
    cos_m, sin_m = cs_ref[:, 2 * LANES:3 * LANES], cs_ref[:, 3 * LANES:4 * LANES]

    for c0 in (0, D_MODEL):
        zf_ref[:, c0:c0 + D_MODEL] = jnp.dot(h, w_ref[:, c0:c0 + D_MODEL], preferred_element_type=jnp.float32)
    for name, (off, width) in ZB.items():
        z = jnp.dot(h, w_ref[:, ZF_GATES + off:ZF_GATES + off + width], preferred_element_type=jnp.float32)
        if name in ROPE_GROUPS:
            z = jnp.concatenate([rope(z[:, c:c + LANES], cos, sin) for c in range(0, width, LANES)], axis=1)
        elif name == "misc":
            z = rope(z, cos_m, sin_m)
        zf_ref[:, ZF_GATES + off:ZF_GATES + off + width] = z
        zb_ref[:, off:off + width] = z.astype(jnp.bfloat16)


def _inproj(x_all, g_mix, w_perm, cs_table, tiles_per_seq, n_prompt_tiles, tm):
    n = x_all.shape[0]

    def cs_map(i):
        return (jnp.where(i < n_prompt_tiles, i % tiles_per_seq, tiles_per_seq + i - n_prompt_tiles), 0)

    return pl.pallas_call(
        _inproj_kernel,
        grid=(n // tm,),
        in_specs=[pl.BlockSpec((tm, D_MODEL), lambda i: (i, 0)),
                  pl.BlockSpec((1, D_MODEL), lambda i: (0, 0)),
                  pl.BlockSpec((D_MODEL, ZF_W), lambda i: (0, 0)),
                  pl.BlockSpec((tm, 4 * LANES), cs_map)],
        out_specs=[pl.BlockSpec((tm, ZF_W), lambda i: (i, 0)),
                   pl.BlockSpec((tm, ZB_W), lambda i: (i, 0))],
        out_shape=[jax.ShapeDtypeStruct((n, ZF_W), jnp.float32),
                   jax.ShapeDtypeStruct((n, ZB_W), jnp.bfloat16)],
        compiler_params=_cparams("parallel"),
        name="inproj",
    )(x_all, g_mix, w_perm, cs_table)


def _permute_w_in(w_in):
    offs = np.cumsum((0,) + PROJ_WIDTHS)
    (qa, ka, va, qi, ki, wi, qb, kcb, vcb, ksb, vsb, kwb, vwb, gb, gate_a, gate_b) = [
        w_in[:, offs[i]:offs[i + 1]] for i in range(len(PROJ_WIDTHS))]
    pad = jnp.zeros((w_in.shape[0], LANES - IDX_DIM - IDX_HEADS - B_HEADS * NSA_BRANCHES), w_in.dtype)
    misc = jnp.concatenate([ki, wi, gb, pad], axis=1)
    w = jnp.concatenate([gate_a, gate_b, qa, qb, qi, ka, va, misc, kcb, vcb, ksb, vsb, kwb, vwb], axis=1)
    return w.astype(jnp.bfloat16)


def _rope_table(pos):
    half = HEAD_DIM // 2
    inv = jnp.power(ROPE_THETA, -jnp.arange(half, dtype=jnp.float32) / half)
    ang = pos.astype(jnp.float32)[:, None] * inv[None, :]
    c, s = jnp.cos(ang), jnp.sin(ang)
    one, zero = jnp.ones_like(c), jnp.zeros_like(c)
    return jnp.concatenate([c, c, c, c, -s, s, -s, s, c, c, one, one, -s, s, zero, zero], axis=1)


def _stack_heads(q, j):
    return jnp.concatenate([q[:, (GQA * j + g) * HEAD_DIM:(GQA * j + g + 1) * HEAD_DIM] for g in range(GQA)], axis=0)


def _flash(qs, k_ref, v_ref, lo, hi, mask_fn, carry0, m_scr, l_scr, acc_scr, tk, fm):
    m_scr[...] = jnp.full(m_scr.shape, NEG, jnp.float32)
    l_scr[...] = jnp.zeros(l_scr.shape, jnp.float32)
    acc_scr[...] = jnp.zeros(acc_scr.shape, jnp.float32)

    def body(ti, carry):
        k0 = pl.multiple_of(ti * tk, tk)
        masks, carry = mask_fn(k0, carry)
        for j in range(A_KV_HEADS):
            if fm:
                kj = k_ref[0, j, :, pl.ds(k0, tk)]
                vj = v_ref[0, j, :, pl.ds(k0, tk)]
                s = jnp.dot(qs[j], kj, preferred_element_type=jnp.float32)
            else:
                kj = k_ref[0, pl.ds(k0, tk), j * HEAD_DIM:(j + 1) * HEAD_DIM]
                vj = v_ref[0, pl.ds(k0, tk), j * HEAD_DIM:(j + 1) * HEAD_DIM]
                s = lax.dot_general(qs[j], kj, _NT, preferred_element_type=jnp.float32)
            mk = jnp.concatenate([masks[j]] * GQA, axis=0)
            s = jnp.where(mk > 0.5, s, NEG)
            m_old = m_scr[j]
            m_new = jnp.maximum(m_old, jnp.max(s, axis=-1, keepdims=True))
            alpha = jnp.exp(m_old - m_new)
            p = jnp.exp(s - m_new) * mk
            l_scr[j] = alpha * l_scr[j] + jnp.sum(p, axis=-1, keepdims=True)
            p16 = p.astype(jnp.bfloat16)
            pv = (lax.dot_general(p16, vj, _NT, preferred_element_type=jnp.float32) if fm
                  else jnp.dot(p16, vj, preferred_element_type=jnp.float32))
            acc_scr[j] = alpha * acc_scr[j] + pv
            m_scr[j] = m_new
        return carry

    lax.fori_loop(lo, hi, body, carry0)
    outs = []
    for j in range(A_KV_HEADS):
        l = l_scr[j]
        outs.append(acc_scr[j] / jnp.where(l > 0.0, l, 1.0))
    return outs


def _flash_scratch(tq):
    r = GQA * tq
    return [pltpu.VMEM((A_KV_HEADS, r, 1), jnp.float32),
            pltpu.VMEM((A_KV_HEADS, r, 1), jnp.float32),
            pltpu.VMEM((A_KV_HEADS, r, HEAD_DIM), jnp.float32)]


def _dsa_kernel(q_ref, qi_ref, misc_ref, ki_ref, k_ref, v_ref, o_ref, key_scr, m_scr, l_scr, acc_scr,
                *, tq, tk, n_key_tiles, pos_base, k_top, fm):
    qb = pl.program_id(1)
    q_lo = pos_base + qb * tq
    nk = jnp.minimum((q_lo + tq + tk - 1) // tk, n_key_tiles)
    q_pos = q_lo + lax.broadcasted_iota(jnp.int32, (tq, 1), 0)
    lane_k = lax.broadcasted_iota(jnp.int32, (tq, tk), 1)

    qi = (qi_ref[0].astype(jnp.float32) * IDX_DIM ** -0.5).astype(jnp.bfloat16)
    wi = misc_ref[0][:, MISC_WI:MISC_WI + IDX_HEADS] * IDX_HEADS ** -0.5

    def score_tile(ti, _):
        k0 = pl.multiple_of(ti * tk, tk)
        kidx = ki_ref[0, :, pl.ds(k0, tk)] if fm else ki_ref[0, pl.ds(k0, tk), 0:IDX_DIM]
        acc = jnp.zeros((tq, tk), jnp.float32)
        for h in range(IDX_HEADS):
            qh = qi[:, h * IDX_DIM:(h + 1) * IDX_DIM]
            s = (jnp.dot(qh, kidx, preferred_element_type=jnp.float32) if fm
                 else lax.dot_general(qh, kidx, _NT, preferred_element_type=jnp.float32))
            acc = acc + jnp.maximum(s, 0.0) * wi[:, h:h + 1]
        bits = pltpu.bitcast(acc, jnp.int32)
        key = jnp.where(bits < 0, bits ^ 0x7FFFFFFF, bits)
        key_scr[:, pl.ds(k0, tk)] = jnp.where(k0 + lane_k <= q_pos, key, INT_MIN)
        return 0

    lax.fori_loop(0, nk, score_tile, 0)

    def count_ge(cand):
        def body(ti, c):
            k0 = pl.multiple_of(ti * tk, tk)
            return c + jnp.where(key_scr[:, pl.ds(k0, tk)] >= cand, 1.0, 0.0)
        c = lax.fori_loop(0, nk, body, jnp.zeros((tq, tk), jnp.float32))
        return jnp.sum(c, axis=-1, keepdims=True)

    def search(i, ans):
        cand = ans + jnp.left_shift(jnp.int32(1), 31 - i)
        return jnp.where(count_ge(cand) >= float(k_top), cand, ans)

    thr = lax.fori_loop(0, 32, search, jnp.full((tq, 1), INT_MIN, jnp.int32))
    need = float(k_top) - count_ge(jnp.where(thr == 2 ** 31 - 1, thr, thr + 1))

    row_i = lax.broadcasted_iota(jnp.int32, (tk, tk), 0)
    col_i = lax.broadcasted_iota(jnp.int32, (tk, tk), 1)
    before = jnp.where(row_i < col_i, 1.0, 0.0).astype(jnp.bfloat16)

    def mask_fn(k0, ties_seen):
        key = key_scr[:, pl.ds(k0, tk)]
        eq = jnp.where(key == thr, 1.0, 0.0)
        rank_eq = jnp.dot(eq.astype(jnp.bfloat16), before, preferred_element_type=jnp.float32) + ties_seen
        take = (key > thr) | ((key == thr) & (rank_eq < need))
        take = take & (key > KEY_NINF) & (key < KEY_PINF)
        mask = jnp.where(take, 1.0, 0.0)
        return [mask, mask], ties_seen + jnp.sum(eq, axis=-1, keepdims=True)

    q = (q_ref[0].astype(jnp.float32) * HEAD_DIM ** -0.5).astype(jnp.bfloat16)
    qs = [_stack_heads(q, j) for j in range(A_KV_HEADS)]
    outs = _flash(qs, k_ref, v_ref, 0, nk, mask_fn, jnp.zeros((tq, 1), jnp.float32), m_scr, l_scr, acc_scr, tk, fm)
    o = jnp.concatenate([outs[j][g * tq:(g + 1) * tq] for j in range(A_KV_HEADS) for g in range(GQA)], axis=1)
    o_ref[0] = o.astype(o_ref.dtype)


def _kv_spec(src, rows, fm):
    if fm:
        shape = (1,) + src[0].shape[1:]
        return pl.BlockSpec(shape, lambda b, i: (b,) + (0,) * (len(shape) - 1))
    return pl.BlockSpec((1, rows, KV_W), lambda b, i, c=src[1]: (b, 0, c))


def _dsa(q_src, qi_src, misc_src, ki_src, k_src, v_src, *, batch, t_q, tq, l_pad, pos_base, k_top, fm, tk=128):
    def qspec(src, width):
        return pl.BlockSpec((1, tq, width), lambda b, i, c=src[1]: (b, i, c))

    kspec = lambda src: _kv_spec(src, l_pad, fm)
    kern = functools.partial(_dsa_kernel, tq=tq, tk=tk, n_key_tiles=l_pad // tk, pos_base=pos_base, k_top=k_top,
                             fm=fm)
    return pl.pallas_call(
        kern,
        grid=(batch, t_q // tq),
        in_specs=[qspec(q_src, A_HEADS * HEAD_DIM), qspec(qi_src, IDX_HEADS * IDX_DIM), qspec(misc_src, LANES),
                  kspec(ki_src), kspec(k_src), kspec(v_src)],
        out_specs=pl.BlockSpec((1, tq, A_HEADS * HEAD_DIM), lambda b, i: (b, i, 0)),
        out_shape=jax.ShapeDtypeStruct((batch, t_q, A_HEADS * HEAD_DIM), jnp.bfloat16),
        scratch_shapes=[pltpu.VMEM((tq, l_pad), jnp.int32)] + _flash_scratch(tq),
        compiler_params=_cparams("parallel", "arbitrary"),
        name="dsa",
    )(q_src[0], qi_src[0], misc_src[0], ki_src[0], k_src[0], v_src[0])


def _compress_kernel(xk_ref, xv_ref, wk_ref, wv_ref, kc_ref, vc_ref, *, nb):
    row = lax.broadcasted_iota(jnp.int32, (nb, KV_W), 0)
    for x_ref, w_ref, o_ref in ((xk_ref, wk_ref, kc_ref), (xv_ref, wv_ref, vc_ref)):
        first = jnp.zeros((nb, KV_W), jnp.float32)
        second = jnp.zeros((nb, KV_W), jnp.float32)
        for j in range(CMP_STRIDE):
            rows = x_ref[0, pl.ds(j, nb, stride=CMP_STRIDE), :].astype(jnp.bfloat16)
            first = first + jnp.dot(rows, w_ref[j], preferred_element_type=jnp.float32)
            second = second + jnp.dot(rows, w_ref[CMP_STRIDE + j], preferred_element_type=jnp.float32)
        tok = first + pltpu.roll(second, nb - 1, 0)
        o_ref[0] = jnp.where(row < nb - 1, tok, 0.0).astype(o_ref.dtype)


def _compress(xk_src, xv_src, wk, wv, *, batch, rows, nb):
    def xspec(src):
        return pl.BlockSpec((1, rows, KV_W), lambda b, c=src[1]: (b, 0, c))

    wspec = pl.BlockSpec((CMP_LEN, KV_W, KV_W), lambda b: (0, 0, 0))
    ospec = pl.BlockSpec((1, nb, KV_W), lambda b: (b, 0, 0))
    oshape = jax.ShapeDtypeStruct((batch, nb, KV_W), jnp.bfloat16)
    return pl.pallas_call(
        functools.partial(_compress_kernel, nb=nb),
        grid=(batch,),
        in_specs=[xspec(xk_src), xspec(xv_src), wspec, wspec],
        out_specs=[ospec, ospec],
        out_shape=[oshape, oshape],
        compiler_params=_cparams("parallel"),
        name="compress",
    )(xk_src[0], xv_src[0], wk, wv)


def _per_head_weights(w_c):
    z = jnp.zeros_like(w_c)
    return jnp.concatenate([jnp.concatenate([w_c, z], axis=2), jnp.concatenate([z, w_c], axis=2)],
                           axis=1).astype(jnp.bfloat16)


def _split3(x):
    hi = x.astype(jnp.bfloat16)
    r1 = x - hi.astype(jnp.float32)
    mid = r1.astype(jnp.bfloat16)
    lo = (r1 - mid.astype(jnp.float32)).astype(jnp.bfloat16)
    return hi, mid, lo


def _nsa_kernel(q_ref, misc_ref, kc_ref, vc_ref, ks_ref, vs_ref, kw_ref, vw_ref, o_ref, m_scr, l_scr, acc_scr,
                *, tq, tk, n_key_tiles, pos_base, n_c, n_s, win_base, n_win_tiles, fm):
    qb = pl.program_id(1)
    q_lo = pos_base + qb * tq
    nk = jnp.minimum((q_lo + tq + tk - 1) // tk, n_key_tiles)
    q_pos = q_lo + lax.broadcasted_iota(jnp.int32, (tq, 1), 0)
    ncp = kc_ref.shape[1]
    nsp = -(-n_s // LANES) * LANES

    q = (q_ref[0].astype(jnp.float32) * HEAD_DIM ** -0.5).astype(jnp.bfloat16)
    qs = [_stack_heads(q, j) for j in range(B_KV_HEADS)]

    n_i = lax.broadcasted_iota(jnp.int32, (tq, ncp), 1)
    c_mask = jnp.where((n_i * CMP_STRIDE + CMP_LEN - 1 <= q_pos) & (n_i < n_c), 1.0, 0.0)
    c_mask4 = jnp.concatenate([c_mask] * GQA, axis=0) > 0.5
    ov_n = lax.broadcasted_iota(jnp.int32, (ncp, nsp), 0) * CMP_STRIDE
    ov_s = lax.broadcasted_iota(jnp.int32, (ncp, nsp), 1) * SEL_BLOCK
    overlap = jnp.where((ov_n < ov_s + SEL_BLOCK) & (ov_n + CMP_LEN > ov_s), 1.0, 0.0).astype(jnp.bfloat16)
    blk = lax.broadcasted_iota(jnp.int32, (tq, nsp), 1)
    forced = ((blk == 0) | (blk == q_pos // SEL_BLOCK)) & (blk < n_s)
    visible = (blk * SEL_BLOCK <= q_pos) & (blk < n_s)
    n_sel = min(SEL_TOPN, n_s)

    o_cmp, sel = [], []
    for j in range(B_KV_HEADS):
        kcj = kc_ref[0, :, j * HEAD_DIM:(j + 1) * HEAD_DIM]
        vcj = vc_ref[0, :, j * HEAD_DIM:(j + 1) * HEAD_DIM]
        s = lax.dot_general(qs[j], kcj, _NT, preferred_element_type=jnp.float32)
        s = jnp.where(c_mask4, s, -jnp.inf)
        m = jnp.max(s, axis=-1, keepdims=True)
        m = jnp.where(m > -jnp.inf, m, 0.0)
        e = jnp.where(c_mask4, jnp.exp(s - m), 0.0)
        den = jnp.sum(e, axis=-1, keepdims=True)
        pc = e / jnp.where(den > 0.0, den, 1.0)
        o_cmp.append(jnp.dot(pc.astype(jnp.bfloat16), vcj, preferred_element_type=jnp.float32))
        pg = pc[0:tq]
        for g in range(1, GQA):
            pg = pg + pc[g * tq:(g + 1) * tq]
        imp = sum(jnp.dot(t, overlap, preferred_element_type=jnp.float32) for t in _split3(pg))
        v = jnp.where(forced, jnp.inf, jnp.where(visible, imp, -jnp.inf))
        rank = jnp.zeros((tq, nsp), jnp.float32)
        for s2 in range(n_s):
            col = v[:, s2:s2 + 1]
            rank = rank + jnp.where((col > v) | ((col == v) & (s2 < blk)), 1.0, 0.0)
        sel.append(jnp.where((rank < float(n_sel)) & (v > -jnp.inf), 1.0, 0.0).astype(jnp.bfloat16))

    lane_k = lax.broadcasted_iota(jnp.int32, (tq, tk), 1)
    e_blk = lax.broadcasted_iota(jnp.int32, (nsp, tk), 0)
    e_key = lax.broadcasted_iota(jnp.int32, (nsp, tk), 1)

    def slc_mask(k0, carry):
        expand = jnp.where(e_blk == (k0 + e_key) // SEL_BLOCK, 1.0, 0.0).astype(jnp.bfloat16)
        causal = k0 + lane_k <= q_pos
        return [jnp.where(causal, jnp.dot(sj, expand, preferred_element_type=jnp.float32), 0.0) for sj in sel], carry

    o_slc = _flash(qs, ks_ref, vs_ref, 0, nk, slc_mask, 0, m_scr, l_scr, acc_scr, tk, fm)

    def win_mask(k0, carry):
        dist = q_pos - (win_base + k0 + lane_k)
        mask = jnp.where((dist >= 0) & (dist < WINDOW) & (win_base + k0 + lane_k >= 0), 1.0, 0.0)
        return [mask, mask], carry

    w_lo = jnp.maximum(q_lo - (WINDOW - 1) - win_base, 0) // tk
    w_hi = jnp.minimum((q_lo + tq - win_base + tk - 1) // tk, n_win_tiles)
    o_win = _flash(qs, kw_ref, vw_ref, w_lo, w_hi, win_mask, 0, m_scr, l_scr, acc_scr, tk, fm)

    gate = jax.nn.sigmoid(misc_ref[0][:, MISC_GB:MISC_GB + B_HEADS * NSA_BRANCHES])
    heads = []
    for j in range(B_KV_HEADS):
        for g in range(GQA):
            c = (GQA * j + g) * NSA_BRANCHES
            rows = slice(g * tq, (g + 1) * tq)
            heads.append(gate[:, c:c + 1] * o_cmp[j][rows] + gate[:, c + 1:c + 2] * o_slc[j][rows]
                         + gate[:, c + 2:c + 3] * o_win[j][rows])
    o_ref[0] = jnp.concatenate(heads, axis=1).astype(o_ref.dtype)


def _nsa(q_src, misc_src, kc, vc, ks_src, vs_src, kw_src, vw_src, *, batch, t_q, tq, l_pad, w_pad, pos_base,
         n_c, n_s, win_base, fm, tk=128):
    def qspec(src, width):
        return pl.BlockSpec((1, tq, width), lambda b, i, c=src[1]: (b, i, c))

    kspec = lambda src, rows: _kv_spec(src, rows, fm)
    cspec = pl.BlockSpec((1,) + kc.shape[1:], lambda b, i: (b, 0, 0))
    kern = functools.partial(_nsa_kernel, tq=tq, tk=tk, n_key_tiles=l_pad // tk, pos_base=pos_base, n_c=n_c,
                             n_s=n_s, win_base=win_base, n_win_tiles=w_pad // tk, fm=fm)
    return pl.pallas_call(
        kern,
        grid=(batch, t_q // tq),
        in_specs=[qspec(q_src, B_HEADS * HEAD_DIM), qspec(misc_src, LANES),
                  cspec, cspec, kspec(ks_src, l_pad), kspec(vs_src, l_pad),
                  kspec(kw_src, w_pad), kspec(vw_src, w_pad)],
        out_specs=pl.BlockSpec((1, tq, B_HEADS * HEAD_DIM), lambda b, i: (b, i, 0)),
        out_shape=jax.ShapeDtypeStruct((batch, t_q, B_HEADS * HEAD_DIM), jnp.bfloat16),
        scratch_shapes=_flash_scratch(tq),
        compiler_params=_cparams("parallel", "arbitrary"),
        name="nsa",
    )(q_src[0], misc_src[0], kc, vc, ks_src[0], vs_src[0], kw_src[0], vw_src[0])


def _gather_kernel(pt_ref, *refs, n_pages, n_arrays, to_rows):
    caches, tails, outs = refs[:n_arrays], refs[n_arrays:2 * n_arrays], refs[2 * n_arrays:]
    p = pl.program_id(1)

    @pl.when(p < n_pages)
    def _():
        for c_ref, o_ref, rows in zip(caches, outs, to_rows):
            page = c_ref[0]
            if rows:
                page = jnp.concatenate([page[h].T for h in range(page.shape[0])], axis=1)
            o_ref[0] = page.astype(o_ref.dtype)

    @pl.when(p == n_pages)
    def _():
        for t_ref, o_ref in zip(tails, outs):
            o_ref[0] = t_ref[0].astype(o_ref.dtype)


def _gather_pages(page_table, caches, tails, dtypes, to_rows):
    batch, n_pages = page_table.shape
    n = len(caches)
    l_pad = (n_pages + 1) * PAGE_SIZE
    zeros = lambda k: (0,) * k
    in_specs = [pl.BlockSpec((1,) + c.shape[1:],
                             lambda b, p, pt, k=c.ndim - 1: (pt[b, jnp.minimum(p, n_pages - 1)],) + zeros(k))
                for c in caches]
    in_specs += [pl.BlockSpec((1,) + t.shape[1:], lambda b, p, pt, k=t.ndim - 1: (b,) + zeros(k)) for t in tails]
    out_specs, out_shape = [], []
    for c, t, dt, rows in zip(caches, tails, dtypes, to_rows):
        if rows:
            out_specs.append(pl.BlockSpec((1, PAGE_SIZE, KV_W), lambda b, p, pt: (b, p, 0)))
            out_shape.append(jax.ShapeDtypeStruct((batch, l_pad, KV_W), dt))
        else:
            out_specs.append(pl.BlockSpec((1,) + c.shape[1:], lambda b, p, pt, k=c.ndim - 2: (b,) + zeros(k) + (p,)))
            out_shape.append(jax.ShapeDtypeStruct((batch,) + c.shape[1:-1] + (l_pad,), dt))
    return pl.pallas_call(
        functools.partial(_gather_kernel, n_pages=n_pages, n_arrays=n, to_rows=tuple(to_rows)),
        grid_spec=pltpu.PrefetchScalarGridSpec(num_scalar_prefetch=1, grid=(batch, n_pages + 1),
                                               in_specs=in_specs, out_specs=out_specs),
        out_shape=out_shape,
        compiler_params=_cparams("parallel", "arbitrary"),
        name="gather_pages",
    )(page_table, *caches, *tails)


def _outproj_kernel(oa_ref, ob_ref, ga_ref, gb_ref, x_ref, wpa_ref, wpb_ref, wo_ref, g_ref, wr_ref, br_ref,
                    x1_ref, h2_ref, gate_ref):
    pa = jnp.dot(oa_ref[...], wpa_ref[...], preferred_element_type=jnp.float32)
    pb = jnp.dot(ob_ref[...], wpb_ref[...], preferred_element_type=jnp.float32)
    merged = jax.nn.sigmoid(ga_ref[...]) * pa + jax.nn.sigmoid(gb_ref[...]) * pb
    x1 = x_ref[...] + jnp.dot(merged.astype(jnp.bfloat16), wo_ref[...], preferred_element_type=jnp.float32)
    x1_ref[...] = x1
    ms = jnp.mean(x1 * x1, axis=-1, keepdims=True)
    h2 = x1 * lax.rsqrt(ms + RMS_EPS) * g_ref[...]
    h2_ref[...] = h2.astype(h2_ref.dtype)

    hs, ws = _split3(h2), _split3(wr_ref[...])
    logits = br_ref[...]
    for a in range(3):
        for b in range(3 - a):
            logits = logits + jnp.dot(hs[a], ws[b], preferred_element_type=jnp.float32)

    tm = logits.shape[0]
    lane = lax.broadcasted_iota(jnp.int32, (tm, N_EXPERTS), 1).astype(jnp.float32)
    work, vals, hots = logits, [], []
    for _ in range(TOP_K):
        m = jnp.max(work, axis=-1, keepdims=True)
        first = jnp.min(jnp.where(work == m, lane, float(N_EXPERTS)), axis=-1, keepdims=True)
        hot = lane == first
        vals.append(m)
        hots.append(hot)
        work = jnp.where(hot, -jnp.inf, work)
    exps = [jnp.exp(v - vals[0]) for v in vals]
    den = exps[0] + exps[1] + exps[2] + exps[3]
    gate = jnp.zeros_like(logits)
    for hot, e in zip(hots, exps):
        gate = gate + jnp.where(hot, e / den, 0.0)
    gate_ref[...] = gate


def _outproj(oa, ob, zf, x_all, wpa, wpb, wo, g_ffn, w_router, b_router, tm):
    n = x_all.shape[0]
    row = lambda w: pl.BlockSpec((tm, w), lambda i: (i, 0))
    full = lambda a: pl.BlockSpec(a.shape, lambda i: (0,) * a.ndim)
    return pl.pallas_call(
        _outproj_kernel,
        grid=(n // tm,),
        in_specs=[row(oa.shape[1]), row(ob.shape[1]),
                  pl.BlockSpec((tm, D_MODEL), lambda i: (i, 0)), pl.BlockSpec((tm, D_MODEL), lambda i: (i, 1)),
                  row(D_MODEL), full(wpa), full(wpb), full(wo), full(g_ffn), full(w_router), full(b_router)],
        out_specs=[row(D_MODEL), row(D_MODEL), row(N_EXPERTS)],
        out_shape=[jax.ShapeDtypeStruct((n, D_MODEL), jnp.float32),
                   jax.ShapeDtypeStruct((n, D_MODEL), jnp.bfloat16),
                   jax.ShapeDtypeStruct((n, N_EXPERTS), jnp.float32)],
        compiler_params=_cparams("parallel"),
        name="outproj_router",
    )(oa, ob, zf, zf, x_all, wpa, wpb, wo, g_ffn, w_router, b_router)


def _moe_kernel(h_ref, gate_ref, x1_ref, wug_ref, bug_ref, wd_ref, bd_ref, gf_ref, y_ref, acc_ref, *, rows):
    e = pl.program_id(1)
    tt = h_ref.shape[0]

    @pl.when(e == 0)
    def _():
        acc_ref[...] = x1_ref[...]

    lane = lax.broadcasted_iota(jnp.int32, (rows, N_EXPERTS), 1)
    for r0 in range(0, tt, rows):
        g = jnp.sum(jnp.where(lane == e, gate_ref[r0:r0 + rows, :], 0.0), axis=-1, keepdims=True)
        ug = jnp.dot(h_ref[r0:r0 + rows, :], wug_ref[0], preferred_element_type=jnp.float32) + bug_ref[0]
        gt = jnp.minimum(ug[:, :D_FF], SWIGLU_LIMIT)
        up = jnp.clip(ug[:, D_FF:], -SWIGLU_LIMIT, SWIGLU_LIMIT)
        act = (up + 1.0) * gt * jax.nn.sigmoid(gt * SWIGLU_ALPHA)
        out = jnp.dot(act.astype(jnp.bfloat16), wd_ref[0], preferred_element_type=jnp.float32) + bd_ref[0]
        acc_ref[r0:r0 + rows, :] += g * out

    @pl.when(e == N_EXPERTS - 1)
    def _():
        x = acc_ref[...]
        ms = jnp.mean(x * x, axis=-1, keepdims=True)
        y_ref[...] = x * lax.rsqrt(ms + RMS_EPS) * gf_ref[...]


MOE_TOKEN_TILE_MAX = 1536


def _largest_tile(n, unit, cap):
    return max(t for t in range(unit, cap + 1, unit) if n % t == 0)


def _moe(h2, gate, x1, w_ug, b_ug, w_down, b_down, g_final, tt):
    n = h2.shape[0]
    rows = max(r for r in (256, 128, 64, 32, 16, 8) if tt % r == 0)
    tok = lambda w: pl.BlockSpec((tt, w), lambda i, e: (i, 0))
    exp = lambda a: pl.BlockSpec((1,) + a.shape[1:], lambda i, e: (e, 0, 0))
    return pl.pallas_call(
        functools.partial(_moe_kernel, rows=rows),
        grid=(n // tt, N_EXPERTS),
        in_specs=[tok(D_MODEL), tok(N_EXPERTS), tok(D_MODEL), exp(w_ug), exp(b_ug), exp(w_down), exp(b_down),
                  pl.BlockSpec((1, D_MODEL), lambda i, e: (0, 0))],
        out_specs=tok(D_MODEL),
        out_shape=jax.ShapeDtypeStruct((n, D_MODEL), jnp.float32),
        scratch_shapes=[pltpu.VMEM((tt, D_MODEL), jnp.float32)],
        compiler_params=_cparams("parallel", "arbitrary"),
        name="moe",
    )(h2, gate, x1, w_ug, b_ug, w_down, b_down, g_final)


def _zf_col(name):
    return (ZF_GATES + ZB[name][0]) // ZB[name][1]


def _zb_col(name):
    return ZB[name][0] // ZB[name][1]


def _layer(x_p, x_s, caches, page_table, w, *, tm=256, tq_prompt=128):
    (c_ak, c_av, c_aik, c_bck, c_bcv, c_bsk, c_bsv, s_wk, s_wv) = caches
    bp, s_len, _ = x_p.shape
    bs, t_new, _ = x_s.shape
    n_p, n_s_tok = bp * s_len, bs * t_new
    past = page_table.shape[1] * PAGE_SIZE
    l_s = past + t_new

    x_all = jnp.concatenate([x_p.reshape(n_p, D_MODEL), x_s.reshape(n_s_tok, D_MODEL)], axis=0)
    pos_table = jnp.concatenate([jnp.arange(s_len), jnp.tile(past + jnp.arange(t_new), bs)])
    zf, zb = _inproj(x_all, w["g_mix"].reshape(1, D_MODEL), _permute_w_in(w["w_in"]), _rope_table(pos_table),
                     s_len // tm, n_p // tm, tm)
    wk = _per_head_weights(w["w_cmp_k"])
    wv = _per_head_weights(w["w_cmp_v"])

    zf_p = zf[:n_p].reshape(bp, s_len, ZF_W)
    zb_p = zb[:n_p].reshape(bp, s_len, ZB_W)
    n_c_p = (s_len - CMP_LEN) // CMP_STRIDE + 1
    kc_p, vc_p = _compress((zf_p, _zf_col("kcb")), (zf_p, _zf_col("vcb")), wk, wv,
                           batch=bp, rows=s_len, nb=n_c_p + 1)
    oa_p = _dsa((zb_p, _zb_col("qa")), (zb_p, _zb_col("qi")), (zf_p, _zf_col("misc")),
                (zb_p, _zb_col("misc")), (zb_p, _zb_col("ka")), (zb_p, _zb_col("va")),
                batch=bp, t_q=s_len, tq=tq_prompt, l_pad=s_len, pos_base=0, k_top=min(IDX_TOPK_MAX, s_len // 4),
                fm=False)
    ob_p = _nsa((zb_p, _zb_col("qb")), (zf_p, _zf_col("misc")), kc_p, vc_p,
                (zb_p, _zb_col("ksb")), (zb_p, _zb_col("vsb")), (zb_p, _zb_col("kwb")), (zb_p, _zb_col("vwb")),
                batch=bp, t_q=s_len, tq=tq_prompt, l_pad=s_len, w_pad=s_len, pos_base=0,
                n_c=n_c_p, n_s=-(-s_len // SEL_BLOCK), win_base=0, fm=False)

    zf_s = zf[n_p:].reshape(bs, t_new, ZF_W)
    new = lambda name: zf_s[:, :, ZF_GATES + ZB[name][0]:ZF_GATES + ZB[name][0] + ZB[name][1]]
    bf, f32 = jnp.bfloat16, jnp.float32

    def fm_page(c):
        return jnp.transpose(c, (0, 2, 3, 1))

    def fm_rows(a, pad_to):
        a = jnp.transpose(a.reshape(bs, a.shape[1], A_KV_HEADS, HEAD_DIM), (0, 2, 3, 1))
        return jnp.pad(a, ((0, 0), (0, 0), (0, 0), (0, pad_to - a.shape[-1])))

    ki_new = jnp.transpose(new("misc")[:, :, :IDX_DIM], (0, 2, 1))
    ki_tail = jnp.pad(ki_new, ((0, 0), (0, 0), (0, PAGE_SIZE - t_new)))
    row_tail = lambda name: jnp.pad(new(name), ((0, 0), (0, PAGE_SIZE - t_new), (0, 0)))
    (ak_f, av_f, aik_f, bck_f, bcv_f, bsk_f, bsv_f) = _gather_pages(
        page_table,
        [fm_page(c_ak), fm_page(c_av), jnp.transpose(c_aik, (0, 2, 1)), fm_page(c_bck), fm_page(c_bcv),
         fm_page(c_bsk), fm_page(c_bsv)],
        [fm_rows(new("ka"), PAGE_SIZE), fm_rows(new("va"), PAGE_SIZE), ki_tail, row_tail("kcb"), row_tail("vcb"),
         fm_rows(new("ksb"), PAGE_SIZE), fm_rows(new("vsb"), PAGE_SIZE)],
        [bf, bf, bf, f32, f32, bf, bf],
        [False, False, False, True, True, False, False])
    l_pad = past + PAGE_SIZE
    n_c_s = (l_s - CMP_LEN) // CMP_STRIDE + 1
    kc_s, vc_s = _compress((bck_f, 0), (bcv_f, 0), wk, wv, batch=bs, rows=l_pad, nb=n_c_s + 1)
    w_buf = s_wk.shape[1]
    w_pad = -(-(w_buf + t_new) // PAGE_SIZE) * PAGE_SIZE
    win = lambda state, name: jnp.concatenate(
        [jnp.transpose(state, (0, 2, 3, 1)), fm_rows(new(name), w_pad - w_buf)], axis=-1)
    kw_all, vw_all = win(s_wk, "kwb"), win(s_wv, "vwb")
    oa_s = _dsa((zf_s, _zf_col("qa")), (zf_s, _zf_col("qi")), (zf_s, _zf_col("misc")),
                (aik_f, 0), (ak_f, 0), (av_f, 0), batch=bs, t_q=t_new, tq=t_new, l_pad=l_pad, pos_base=past,
                k_top=min(IDX_TOPK_MAX, l_s // 4), fm=True)
    ob_s = _nsa((zf_s, _zf_col("qb")), (zf_s, _zf_col("misc")), kc_s, vc_s, (bsk_f, 0), (bsv_f, 0),
                (kw_all.astype(bf), 0), (vw_all.astype(bf), 0),
                batch=bs, t_q=t_new, tq=t_new, l_pad=l_pad, w_pad=w_pad, pos_base=past,
                n_c=n_c_s, n_s=-(-l_s // SEL_BLOCK), win_base=past - w_buf, fm=True)

    oa = jnp.concatenate([oa_p.reshape(n_p, -1), oa_s.reshape(n_s_tok, -1)], axis=0)
    ob = jnp.concatenate([ob_p.reshape(n_p, -1), ob_s.reshape(n_s_tok, -1)], axis=0)
    x1, h2, gate = _outproj(oa, ob, zf, x_all, w["w_proj_a"].astype(bf), w["w_proj_b"].astype(bf),
                            w["w_out"].astype(bf), w["g_ffn"].reshape(1, D_MODEL), w["w_router"],
                            w["b_router"].reshape(1, N_EXPERTS), tm)
    y = _moe(h2, gate, x1, w["w_ug"].astype(bf), w["b_ug"].reshape(N_EXPERTS, 1, 2 * D_FF),
             w["w_down"].astype(bf), w["b_down"].reshape(N_EXPERTS, 1, D_MODEL), w["g_final"].reshape(1, D_MODEL),
             tt=_largest_tile(n_p + n_s_tok, tm, MOE_TOKEN_TILE_MAX))

    heads = lambda a: a.reshape(a.shape[0], a.shape[1], A_KV_HEADS, HEAD_DIM)
    col_p = lambda name: zf_p[:, :, ZF_GATES + ZB[name][0]:ZF_GATES + ZB[name][0] + ZB[name][1]]
    w_keep = min(WINDOW, s_len)
    st_p = (heads(col_p("ka")), heads(col_p("va")), col_p("misc")[:, :, :IDX_DIM], heads(col_p("kcb")),
            heads(col_p("vcb")), heads(col_p("ksb")), heads(col_p("vsb")),
            heads(col_p("kwb")[:, s_len - w_keep:]), heads(col_p("vwb")[:, s_len - w_keep:]))
    st_s = (heads(new("ka")), heads(new("va")), new("misc")[:, :, :IDX_DIM], heads(new("kcb")), heads(new("vcb")),
            heads(new("ksb")), heads(new("vsb")),
            jnp.transpose(kw_all[..., t_new:t_new + w_buf], (0, 3, 1, 2)),
            jnp.transpose(vw_all[..., t_new:t_new + w_buf], (0, 3, 1, 2)))
    return y[:n_p].reshape(bp, s_len, D_MODEL), y[n_p:].reshape(bs, t_new, D_MODEL), st_p, st_s


def kernel(x_prompt, x_sample, cache_a_k, cache_a_v, cache_a_idx_k, cache_b_cmp_k, cache_b_cmp_v, cache_b_slc_k,
           cache_b_slc_v, state_b_win_k, state_b_win_v, page_table, g_mix, w_in, w_cmp_k, w_cmp_v, w_proj_a,
           w_proj_b, w_out, g_ffn, w_router, b_router, w_ug, b_ug, w_down, b_down, g_final):
    depth = g_mix.shape[0]
    assert depth == 1, "the final rmsnorm is fused into the last layer's expert kernel; one layer is supported"
    caches = (cache_a_k[0], cache_a_v[0], cache_a_idx_k[0], cache_b_cmp_k[0], cache_b_cmp_v[0], cache_b_slc_k[0],
              cache_b_slc_v[0], state_b_win_k[0], state_b_win_v[0])
    w = dict(g_mix=g_mix[0], w_in=w_in[0], w_cmp_k=w_cmp_k[0], w_cmp_v=w_cmp_v[0], w_proj_a=w_proj_a[0],
             w_proj_b=w_proj_b[0], w_out=w_out[0], g_ffn=g_ffn[0], w_router=w_router[0], b_router=b_router[0],
             w_ug=w_ug[0], b_ug=b_ug[0], w_down=w_down[0], b_down=b_down[0], g_final=g_final)
    y_p, y_s, st_p, st_s = _layer(x_prompt, x_sample, caches, page_table, w)
    return (y_p, y_s) + tuple(s[None] for s in st_p) + tuple(s[None] for s in st_s)
```

```python
import functools

import numpy as np
import jax
import jax.numpy as jnp
from jax import lax
from jax.experimental import pallas as pl
from jax.experimental.pallas import tpu as pltpu

D_MODEL = 1024
HEAD_DIM = 64
PAGE_SIZE = 128
A_HEADS = 8
A_KV_HEADS = 2
IDX_HEADS = 4
IDX_DIM = 64
IDX_TOPK_MAX = 256
B_HEADS = 8
B_KV_HEADS = 2
NSA_BRANCHES = 3
CMP_LEN = 32
CMP_STRIDE = 16
SEL_BLOCK = 64
SEL_TOPN = 16
WINDOW = 512
N_EXPERTS = 32
TOP_K = 4
D_FF = 1024
SWIGLU_LIMIT = 7.0
SWIGLU_ALPHA = 1.702
ROPE_THETA = 10000.0
RMS_EPS = 1e-5
PROJ_WIDTHS = (A_HEADS * HEAD_DIM, A_KV_HEADS * HEAD_DIM, A_KV_HEADS * HEAD_DIM, IDX_HEADS * IDX_DIM, IDX_DIM,
               IDX_HEADS, B_HEADS * HEAD_DIM, B_KV_HEADS * HEAD_DIM, B_KV_HEADS * HEAD_DIM, B_KV_HEADS * HEAD_DIM,
               B_KV_HEADS * HEAD_DIM, B_KV_HEADS * HEAD_DIM, B_KV_HEADS * HEAD_DIM, B_HEADS * NSA_BRANCHES,
               D_MODEL, D_MODEL)

LANES = 128
GQA = A_HEADS // A_KV_HEADS
KV_W = A_KV_HEADS * HEAD_DIM
VMEM_LIMIT = 48 * 1024 * 1024

ZB = dict(qa=(0, 512), qb=(512, 512), qi=(1024, 256), ka=(1280, 128), va=(1408, 128), misc=(1536, 128),
          kcb=(1664, 128), vcb=(1792, 128), ksb=(1920, 128), vsb=(2048, 128), kwb=(2176, 128), vwb=(2304, 128))
ZB_W = 2432
ZF_GATES = 2 * D_MODEL
ZF_W = ZF_GATES + ZB_W
MISC_WI = IDX_DIM
MISC_GB = IDX_DIM + IDX_HEADS
ROPE_GROUPS = ("qa", "qb", "qi", "ka", "kcb", "ksb", "kwb")

NEG = -1e30
Q_SCALE = HEAD_DIM ** -0.5 * float(np.log2(np.e))
INT_MIN = -2 ** 31
KEY_PINF = 0x7F800000
KEY_NINF = -0x7F800001

_NT = (((1,), (1,)), ((), ()))


def _cparams(*sem):
    return pltpu.CompilerParams(dimension_semantics=sem, vmem_limit_bytes=VMEM_LIMIT)


def _inproj_kernel(x_ref, g_ref, w_ref, cs_ref, zf_ref, zb_ref):
    x = x_ref[...]
    ms = jnp.mean(x * x, axis=-1, keepdims=True)
    h = (x * lax.rsqrt(ms + RMS_EPS) * g_ref[...]).astype(jnp.bfloat16)
    tm = x.shape[0]
    lane = lax.broadcasted_iota(jnp.int32, (tm, LANES), 1)
    first_half = (lane % HEAD_DIM) < (HEAD_DIM // 2)

    def rope(z, c, s):
        partner = jnp.where(first_half, pltpu.roll(z, LANES - HEAD_DIM // 2, 1), pltpu.roll(z, HEAD_DIM // 2, 1))
        return z * c + partner * s

    cos, sin = cs_ref[:, 0:LANES], cs_ref[:, LANES:2 * LANES]
    cos_m, sin_m = cs_ref[:, 2 * LANES:3 * LANES], cs_ref[:, 3 * LANES:4 * LANES]

    for c0 in (0, D_MODEL):
        zf_ref[:, c0:c0 + D_MODEL] = jnp.dot(h, w_ref[:, c0:c0 + D_MODEL], preferred_element_type=jnp.float32)
    for name, (off, width) in ZB.items():
        z = jnp.dot(h, w_ref[:, ZF_GATES + off:ZF_GATES + off + width], preferred_element_type=jnp.float32)
        if name in ROPE_GROUPS:
            z = jnp.concatenate([rope(z[:, c:c + LANES], cos, sin) for c in range(0, width, LANES)], axis=1)
        elif name == "misc":
            z = rope(z, cos_m, sin_m)
        zf_ref[:, ZF_GATES + off:ZF_GATES + off + width] = z
        zb_ref[:, off:off + width] = z.astype(jnp.bfloat16)


def _inproj(x_all, g_mix, w_perm, cs_table, tiles_per_seq, n_prompt_tiles, tm):
    n = x_all.shape[0]

    def cs_map(i):
        return (jnp.where(i < n_prompt_tiles, i % tiles_per_seq, tiles_per_seq + i - n_prompt_tiles), 0)

    return pl.pallas_call(
        _inproj_kernel,
        grid=(n // tm,),
        in_specs=[pl.BlockSpec((tm, D_MODEL), lambda i: (i, 0)),
                  pl.BlockSpec((1, D_MODEL), lambda i: (0, 0)),
                  pl.BlockSpec((D_MODEL, ZF_W), lambda i: (0, 0)),
                  pl.BlockSpec((tm, 4 * LANES), cs_map)],
        out_specs=[pl.BlockSpec((tm, ZF_W), lambda i: (i, 0)),
                   pl.BlockSpec((tm, ZB_W), lambda i: (i, 0))],
        out_shape=[jax.ShapeDtypeStruct((n, ZF_W), jnp.float32),
                   jax.ShapeDtypeStruct((n, ZB_W), jnp.bfloat16)],
        compiler_params=_cparams("parallel"),
        name="inproj",
    )(x_all, g_mix, w_perm, cs_table)


HEAD_PAIR_ORDER = tuple(h for g in range(GQA) for h in (g, GQA + g))


def _pair_head_cols(w):
    return w.reshape(w.shape[0], A_HEADS, HEAD_DIM)[:, HEAD_PAIR_ORDER, :].reshape(w.shape)


def _pair_head_rows(w):
    return w.reshape(A_HEADS, HEAD_DIM, w.shape[1])[HEAD_PAIR_ORDER, :, :].reshape(w.shape)


def _permute_w_in(w_in):
    offs = np.cumsum((0,) + PROJ_WIDTHS)
    (qa, ka, va, qi, ki, wi, qb, kcb, vcb, ksb, vsb, kwb, vwb, gb, gate_a, gate_b) = [
        w_in[:, offs[i]:offs[i + 1]] for i in range(len(PROJ_WIDTHS))]
    pad = jnp.zeros((w_in.shape[0], LANES - IDX_DIM - IDX_HEADS - B_HEADS * NSA_BRANCHES), w_in.dtype)
    misc = jnp.concatenate([ki, wi, gb, pad], axis=1)
    qa, qb = _pair_head_cols(qa), _pair_head_cols(qb)
    w = jnp.concatenate([gate_a, gate_b, qa, qb, qi, ka, va, misc, kcb, vcb, ksb, vsb, kwb, vwb], axis=1)
    return w.astype(jnp.bfloat16)


def _rope_table(pos):
    half = HEAD_DIM // 2
    inv = jnp.power(ROPE_THETA, -jnp.arange(half, dtype=jnp.float32) / half)
    ang = pos.astype(jnp.float32)[:, None] * inv[None, :]
    c, s = jnp.cos(ang), jnp.sin(ang)
    one, zero = jnp.ones_like(c), jnp.zeros_like(c)
    return jnp.concatenate([c, c, c, c, -s, s, -s, s, c, c, one, one, -s, s, zero, zero], axis=1)


def _stack_queries(q):
    tq = q.shape[0]
    low = lax.broadcasted_iota(jnp.int32, (tq, LANES), 1) < HEAD_DIM
    chunks = [q[:, g * LANES:(g + 1) * LANES] for g in range(GQA)]
    zero = jnp.zeros_like(chunks[0])
    return jnp.concatenate([jnp.where(low, c, zero) for c in chunks] + [jnp.where(low, zero, c) for c in chunks],
                           axis=0).astype(jnp.bfloat16)


def _pair_heads(o, tq):
    low = lax.broadcasted_iota(jnp.int32, (tq, LANES), 1) < HEAD_DIM
    return [jnp.where(low, o[g * tq:(g + 1) * tq], o[(GQA + g) * tq:(GQA + g + 1) * tq]) for g in range(GQA)]


def _flash(q_all, k_ref, v_ref, lo, hi, bias_fn, carry0, m_scr, l_scr, acc_scr, tk, fm):
    m_scr[...] = jnp.full(m_scr.shape, NEG, jnp.float32)
    l_scr[...] = jnp.zeros(l_scr.shape, jnp.float32)
    acc_scr[...] = jnp.zeros(acc_scr.shape, jnp.float32)
    n_chunks = tk // LANES

    def body(ti, carry):
        k0 = pl.multiple_of(ti * tk, tk)
        biases, carry = bias_fn(k0, carry)
        bias = jnp.concatenate([biases[0]] * GQA + [biases[1]] * GQA, axis=0)
        if fm:
            s = jnp.dot(q_all, k_ref[0, :, pl.ds(k0, tk)], preferred_element_type=jnp.float32)
        else:
            s = lax.dot_general(q_all, k_ref[0, pl.ds(k0, tk), :], _NT, preferred_element_type=jnp.float32)
        s = s + bias
        chunks = [s[:, c * LANES:(c + 1) * LANES] for c in range(n_chunks)]
        cm = chunks[0]
        for c in chunks[1:]:
            cm = jnp.maximum(cm, c)
        m_old = m_scr[...]
        m_new = jnp.maximum(m_old, jnp.max(cm, axis=-1, keepdims=True))
        alpha = jnp.exp2(m_old - m_new)
        ps = [jnp.exp2(c - m_new) for c in chunks]
        psum = ps[0]
        for p in ps[1:]:
            psum = psum + p
        l_scr[...] = alpha * l_scr[...] + jnp.sum(psum, axis=-1, keepdims=True)
        p16 = jnp.concatenate(ps, axis=1).astype(jnp.bfloat16)
        if fm:
            pv = lax.dot_general(p16, v_ref[0, :, pl.ds(k0, tk)], _NT, preferred_element_type=jnp.float32)
        else:
            pv = jnp.dot(p16, v_ref[0, pl.ds(k0, tk), :], preferred_element_type=jnp.float32)
        acc_scr[...] = alpha * acc_scr[...] + pv
        m_scr[...] = m_new
        return carry

    lax.fori_loop(lo, hi, body, carry0)
    return jnp.where(m_scr[...] > 0.5 * NEG, acc_scr[...] / l_scr[...], 0.0)


def _flash_scratch(tq):
    return [pltpu.VMEM((2 * GQA * tq, LANES), jnp.float32) for _ in range(3)]


def _kv_spec(src, rows, fm):
    if fm:
        return pl.BlockSpec((1,) + src[0].shape[1:], lambda b, i: (b, 0, 0))
    return pl.BlockSpec((1, rows, KV_W), lambda b, i, c=src[1]: (b, 0, c))


def _dsa_kernel(q_ref, qi_ref, misc_ref, ki_ref, k_ref, v_ref, o_ref, key_scr, m_scr, l_scr, acc_scr,
                *, tq, tk, n_key_tiles, pos_base, k_top, fm):
    qb = pl.program_id(1)
    q_lo = pos_base + qb * tq
    nk = jnp.minimum((q_lo + tq + tk - 1) // tk, n_key_tiles)
    q_pos = q_lo + lax.broadcasted_iota(jnp.int32, (tq, 1), 0)
    lane_k = lax.broadcasted_iota(jnp.int32, (tq, tk), 1)
    n_chunks = tk // LANES

    qi = qi_ref[0].astype(jnp.float32) * IDX_DIM ** -0.5
    if fm:
        qh = [qi[:, h * IDX_DIM:(h + 1) * IDX_DIM].astype(jnp.bfloat16) for h in range(IDX_HEADS)]
    else:
        low = lax.broadcasted_iota(jnp.int32, (tq, LANES), 1) < IDX_DIM
        qh = []
        for h in range(IDX_HEADS):
            c = qi[:, (h // 2) * LANES:(h // 2 + 1) * LANES]
            if h % 2:
                c = pltpu.roll(c, IDX_DIM, 1)
            qh.append(jnp.where(low, c, 0.0).astype(jnp.bfloat16))
    wi = misc_ref[0][:, MISC_WI:MISC_WI + IDX_HEADS] * IDX_HEADS ** -0.5
    wi_b = [jnp.broadcast_to(wi[:, h:h + 1], (tq, tk)) for h in range(IDX_HEADS)]

    def score_tile(ti, _):
        k0 = pl.multiple_of(ti * tk, tk)
        acc = jnp.zeros((tq, tk), jnp.float32)
        for h in range(IDX_HEADS):
            if fm:
                s = jnp.dot(qh[h], ki_ref[0, :, pl.ds(k0, tk)], preferred_element_type=jnp.float32)
            else:
                s = lax.dot_general(qh[h], ki_ref[0, pl.ds(k0, tk), :], _NT, preferred_element_type=jnp.float32)
            acc = acc + jnp.maximum(s, 0.0) * wi_b[h]
        bits = pltpu.bitcast(acc, jnp.int32)
        key = jnp.where(bits < 0, bits ^ 0x7FFFFFFF, bits)
        key_scr[:, pl.ds(k0, tk)] = jnp.where(k0 + lane_k <= q_pos, key, INT_MIN)
        return 0

    lax.fori_loop(0, nk, score_tile, 0)

    def count_ge(cand):
        cand_b = jnp.broadcast_to(cand, (tq, LANES))

        def body(ti, c):
            k0 = pl.multiple_of(ti * tk, tk)
            for j in range(n_chunks):
                c = c + jnp.where(key_scr[:, pl.ds(k0 + j * LANES, LANES)] >= cand_b, 1.0, 0.0)
            return c
        c = lax.fori_loop(0, nk, body, jnp.zeros((tq, LANES), jnp.float32))
        return jnp.sum(c, axis=-1, keepdims=True)

    def search(i, ans):
        cand = ans + jnp.left_shift(jnp.int32(1), 31 - i)
        return jnp.where(count_ge(cand) >= float(k_top), cand, ans)

    thr = lax.fori_loop(0, 32, search, jnp.full((tq, 1), INT_MIN, jnp.int32))
    need = float(k_top) - count_ge(jnp.where(thr == 2 ** 31 - 1, thr, thr + 1))

    row_i = lax.broadcasted_iota(jnp.int32, (tk, tk), 0)
    col_i = lax.broadcasted_iota(jnp.int32, (tk, tk), 1)
    before = jnp.where(row_i < col_i, 1.0, 0.0).astype(jnp.bfloat16)

    def bias_fn(k0, ties_seen):
        key = key_scr[:, pl.ds(k0, tk)]
        eq = jnp.where(key == thr, 1.0, 0.0)
        rank_eq = jnp.dot(eq.astype(jnp.bfloat16), before, preferred_element_type=jnp.float32) + ties_seen
        take = (key > thr) | ((key == thr) & (rank_eq < need))
        take = take & (key > KEY_NINF) & (key < KEY_PINF)
        bias = jnp.where(take, 0.0, NEG)
        return [bias, bias], ties_seen + jnp.sum(eq, axis=-1, keepdims=True)

    q_all = _stack_queries(q_ref[0].astype(jnp.float32) * Q_SCALE)
    o = _flash(q_all, k_ref, v_ref, 0, nk, bias_fn, jnp.zeros((tq, 1), jnp.float32), m_scr, l_scr, acc_scr, tk, fm)
    o_ref[0] = jnp.concatenate(_pair_heads(o, tq), axis=1).astype(o_ref.dtype)


def _dsa(q_src, qi_src, misc_src, ki_src, k_src, v_src, *, batch, t_q, tq, l_pad, pos_base, k_top, fm, tk):
    def qspec(src, width):
        return pl.BlockSpec((1, tq, width), lambda b, i, c=src[1]: (b, i, c))

    kspec = lambda src: _kv_spec(src, l_pad, fm)
    kern = functools.partial(_dsa_kernel, tq=tq, tk=tk, n_key_tiles=l_pad // tk, pos_base=pos_base, k_top=k_top,
                             fm=fm)
    return pl.pallas_call(
        kern,
        grid=(batch, t_q // tq),
        in_specs=[qspec(q_src, A_HEADS * HEAD_DIM), qspec(qi_src, IDX_HEADS * IDX_DIM), qspec(misc_src, LANES),
                  kspec(ki_src), kspec(k_src), kspec(v_src)],
        out_specs=pl.BlockSpec((1, tq, A_HEADS * HEAD_DIM), lambda b, i: (b, i, 0)),
        out_shape=jax.ShapeDtypeStruct((batch, t_q, A_HEADS * HEAD_DIM), jnp.bfloat16),
        scratch_shapes=[pltpu.VMEM((tq, l_pad), jnp.int32)] + _flash_scratch(tq),
        compiler_params=_cparams("parallel", "arbitrary"),
        name="dsa",
    )(q_src[0], qi_src[0], misc_src[0], ki_src[0], k_src[0], v_src[0])


def _compress_kernel(xk_ref, xv_ref, wk_ref, wv_ref, kc_ref, vc_ref, *, nb):
    row = lax.broadcasted_iota(jnp.int32, (nb, KV_W), 0)
    for x_ref, w_ref, o_ref in ((xk_ref, wk_ref, kc_ref), (xv_ref, wv_ref, vc_ref)):
        first = jnp.zeros((nb, KV_W), jnp.float32)
        second = jnp.zeros((nb, KV_W), jnp.float32)
        for j in range(CMP_STRIDE):
            rows = x_ref[0, pl.ds(j, nb, stride=CMP_STRIDE), :].astype(jnp.bfloat16)
            first = first + jnp.dot(rows, w_ref[j], preferred_element_type=jnp.float32)
            second = second + jnp.dot(rows, w_ref[CMP_STRIDE + j], preferred_element_type=jnp.float32)
        tok = first + pltpu.roll(second, nb - 1, 0)
        o_ref[0] = jnp.where(row < nb - 1, tok, 0.0).astype(o_ref.dtype)


def _compress(xk_src, xv_src, wk, wv, *, batch, rows, nb):
    def xspec(src):
        return pl.BlockSpec((1, rows, KV_W), lambda b, c=src[1]: (b, 0, c))

    wspec = pl.BlockSpec((CMP_LEN, KV_W, KV_W), lambda b: (0, 0, 0))
    ospec = pl.BlockSpec((1, nb, KV_W), lambda b: (b, 0, 0))
    oshape = jax.ShapeDtypeStruct((batch, nb, KV_W), jnp.bfloat16)
    return pl.pallas_call(
        functools.partial(_compress_kernel, nb=nb),
        grid=(batch,),
        in_specs=[xspec(xk_src), xspec(xv_src), wspec, wspec],
        out_specs=[ospec, ospec],
        out_shape=[oshape, oshape],
        compiler_params=_cparams("parallel"),
        name="compress",
    )(xk_src[0], xv_src[0], wk, wv)


def _per_head_weights(w_c):
    z = jnp.zeros_like(w_c)
    return jnp.concatenate([jnp.concatenate([w_c, z], axis=2), jnp.concatenate([z, w_c], axis=2)],
                           axis=1).astype(jnp.bfloat16)


def _split3(x):
    hi = x.astype(jnp.bfloat16)
    r1 = x - hi.astype(jnp.float32)
    mid = r1.astype(jnp.bfloat16)
    lo = (r1 - mid.astype(jnp.float32)).astype(jnp.bfloat16)
    return hi, mid, lo


def _nsa_kernel(q_ref, misc_ref, kc_ref, vc_ref, ks_ref, vs_ref, kw_ref, vw_ref, o_ref, m_scr, l_scr, acc_scr,
                *, tq, tk, n_key_tiles, pos_base, n_c, n_s, win_base, n_win_tiles, fm):
    qb = pl.program_id(1)
    q_lo = pos_base + qb * tq
    nk = jnp.minimum((q_lo + tq + tk - 1) // tk, n_key_tiles)
    q_pos = q_lo + lax.broadcasted_iota(jnp.int32, (tq, 1), 0)
    ncp = kc_ref.shape[1]
    n_sel = min(SEL_TOPN, n_s)
    paired = n_s <= HEAD_DIM
    bw = HEAD_DIM if paired else -(-n_s // LANES) * LANES
    vw = LANES if paired else bw

    q_all = _stack_queries(q_ref[0].astype(jnp.float32) * Q_SCALE)

    n_i = lax.broadcasted_iota(jnp.int32, (tq, ncp), 1)
    c_ok = (n_i * CMP_STRIDE + CMP_LEN - 1 <= q_pos) & (n_i < n_c)
    c_mask = jnp.concatenate([jnp.where(c_ok, 1.0, 0.0)] * (2 * GQA), axis=0) > 0.5
    s = lax.dot_general(q_all, kc_ref[0], _NT, preferred_element_type=jnp.float32)
    s = jnp.where(c_mask, s, -jnp.inf)
    m = jnp.max(s, axis=-1, keepdims=True)
    m = jnp.where(m > -jnp.inf, m, 0.0)
    e = jnp.where(c_mask, jnp.exp2(s - m), 0.0)
    den = jnp.sum(e, axis=-1, keepdims=True)
    pc = e / jnp.where(den > 0.0, den, 1.0)
    o_cmp = jnp.dot(pc.astype(jnp.bfloat16), vc_ref[0], preferred_element_type=jnp.float32)

    pg = []
    for j in range(B_KV_HEADS):
        t = pc[GQA * j * tq:(GQA * j + 1) * tq]
        for g in range(1, GQA):
            t = t + pc[(GQA * j + g) * tq:(GQA * j + g + 1) * tq]
        pg.append(t)
    sel = []
    if paired:
        nb8 = -(-n_s // 8) * 8
        ov_s = lax.broadcasted_iota(jnp.int32, (nb8, ncp), 0) * SEL_BLOCK
        ov_n = lax.broadcasted_iota(jnp.int32, (nb8, ncp), 1) * CMP_STRIDE
        overlap_t = jnp.where((ov_n < ov_s + SEL_BLOCK) & (ov_n + CMP_LEN > ov_s), 1.0, 0.0).astype(jnp.bfloat16)
        blk_t = lax.broadcasted_iota(jnp.int32, (nb8, tq), 0)
        pos_t = q_lo + lax.broadcasted_iota(jnp.int32, (nb8, tq), 1)
        forced_t = ((blk_t == 0) | (blk_t == pos_t // SEL_BLOCK)) & (blk_t < n_s)
        visible_t = (blk_t * SEL_BLOCK <= pos_t) & (blk_t < n_s)
        lane_v = lax.broadcasted_iota(jnp.int32, (tq, LANES), 1)
        for j in range(B_KV_HEADS):
            imp = sum(lax.dot_general(overlap_t, t, _NT, preferred_element_type=jnp.float32) for t in _split3(pg[j]))
            v = jnp.where(forced_t, jnp.inf, jnp.where(visible_t, imp, -jnp.inf))
            rank = jnp.zeros((nb8, tq), jnp.float32)
            for s2 in range(n_s):
                row = jnp.broadcast_to(v[s2:s2 + 1, :], (nb8, tq))
                rank = rank + jnp.where((row > v) | ((row == v) & (s2 < blk_t)), 1.0, 0.0)
            chosen = jnp.where((rank < float(n_sel)) & (v > -jnp.inf), 1.0, 0.0).T
            chosen = jnp.concatenate([chosen, jnp.zeros((tq, bw - nb8), jnp.float32)], axis=1) if bw > nb8 else chosen
            zero = jnp.zeros_like(chosen)
            sel.append(jnp.concatenate([chosen, zero] if j == 0 else [zero, chosen], axis=1))
    else:
        ov_n = lax.broadcasted_iota(jnp.int32, (ncp, vw), 0) * CMP_STRIDE
        ov_s = lax.broadcasted_iota(jnp.int32, (ncp, vw), 1) * SEL_BLOCK
        overlap = jnp.where((ov_n < ov_s + SEL_BLOCK) & (ov_n + CMP_LEN > ov_s), 1.0, 0.0).astype(jnp.bfloat16)
        blk = lax.broadcasted_iota(jnp.int32, (tq, vw), 1)
        forced = ((blk == 0) | (blk == q_pos // SEL_BLOCK)) & (blk < n_s)
        visible = (blk * SEL_BLOCK <= q_pos) & (blk < n_s)
        for j in range(B_KV_HEADS):
            imp = sum(jnp.dot(t, overlap, preferred_element_type=jnp.float32) for t in _split3(pg[j]))
            v = jnp.where(forced, jnp.inf, jnp.where(visible, imp, -jnp.inf))
            rank = jnp.zeros((tq, vw), jnp.float32)
            for s2 in range(n_s):
                col = jnp.broadcast_to(v[:, s2:s2 + 1], (tq, vw))
                rank = rank + jnp.where((col > v) | ((col == v) & (s2 < blk)), 1.0, 0.0)
            sel.append(jnp.where((rank < float(n_sel)) & (v > -jnp.inf), 1.0, 0.0))
    sel = [t.astype(jnp.bfloat16) for t in sel]

    lane_k = lax.broadcasted_iota(jnp.int32, (tq, tk), 1)
    e_blk = lax.broadcasted_iota(jnp.int32, (vw, tk), 0) % bw
    e_key = lax.broadcasted_iota(jnp.int32, (vw, tk), 1)

    def slc_bias(k0, carry):
        expand = jnp.where(e_blk == (k0 + e_key) // SEL_BLOCK, 1.0, 0.0).astype(jnp.bfloat16)
        causal = k0 + lane_k <= q_pos
        return [jnp.where(causal & (jnp.dot(sj, expand, preferred_element_type=jnp.float32) > 0.5), 0.0, NEG)
                for sj in sel], carry

    o_slc = _flash(q_all, ks_ref, vs_ref, 0, nk, slc_bias, 0, m_scr, l_scr, acc_scr, tk, fm)

    def win_bias(k0, carry):
        k_pos = win_base + k0 + lane_k
        dist = q_pos - k_pos
        bias = jnp.where((dist >= 0) & (dist < WINDOW) & (k_pos >= 0), 0.0, NEG)
        return [bias, bias], carry

    w_lo = jnp.maximum(q_lo - (WINDOW - 1) - win_base, 0) // tk
    w_hi = jnp.minimum((q_lo + tq - win_base + tk - 1) // tk, n_win_tiles)
    o_win = _flash(q_all, kw_ref, vw_ref, w_lo, w_hi, win_bias, 0, m_scr, l_scr, acc_scr, tk, fm)

    gate = jax.nn.sigmoid(misc_ref[0][:, MISC_GB:MISC_GB + B_HEADS * NSA_BRANCHES])
    low = lax.broadcasted_iota(jnp.int32, (tq, LANES), 1) < HEAD_DIM
    branches = [_pair_heads(o, tq) for o in (o_cmp, o_slc, o_win)]
    chunks = []
    for g in range(GQA):
        acc = jnp.zeros((tq, LANES), jnp.float32)
        for r in range(NSA_BRANCHES):
            c0, c1 = g * NSA_BRANCHES + r, (GQA + g) * NSA_BRANCHES + r
            gr = jnp.where(low, jnp.broadcast_to(gate[:, c0:c0 + 1], (tq, LANES)),
                           jnp.broadcast_to(gate[:, c1:c1 + 1], (tq, LANES)))
            acc = acc + gr * branches[r][g]
        chunks.append(acc)
    o_ref[0] = jnp.concatenate(chunks, axis=1).astype(o_ref.dtype)


def _nsa(q_src, misc_src, kc, vc, ks_src, vs_src, kw_src, vw_src, *, batch, t_q, tq, l_pad, w_pad, pos_base,
         n_c, n_s, win_base, fm, tk):
    def qspec(src, width):
        return pl.BlockSpec((1, tq, width), lambda b, i, c=src[1]: (b, i, c))

    kspec = lambda src, rows: _kv_spec(src, rows, fm)
    cspec = pl.BlockSpec((1,) + kc.shape[1:], lambda b, i: (b, 0, 0))
    kern = functools.partial(_nsa_kernel, tq=tq, tk=tk, n_key_tiles=l_pad // tk, pos_base=pos_base, n_c=n_c,
                             n_s=n_s, win_base=win_base, n_win_tiles=w_pad // tk, fm=fm)
    return pl.pallas_call(
        kern,
        grid=(batch, t_q // tq),
        in_specs=[qspec(q_src, B_HEADS * HEAD_DIM), qspec(misc_src, LANES),
                  cspec, cspec, kspec(ks_src, l_pad), kspec(vs_src, l_pad),
                  kspec(kw_src, w_pad), kspec(vw_src, w_pad)],
        out_specs=pl.BlockSpec((1, tq, B_HEADS * HEAD_DIM), lambda b, i: (b, i, 0)),
        out_shape=jax.ShapeDtypeStruct((batch, t_q, B_HEADS * HEAD_DIM), jnp.bfloat16),
        scratch_shapes=_flash_scratch(tq),
        compiler_params=_cparams("parallel", "arbitrary"),
        name="nsa",
    )(q_src[0], misc_src[0], kc, vc, ks_src[0], vs_src[0], kw_src[0], vw_src[0])


def _gather_kernel(pt_ref, *refs, n_steps, n_arrays, pps, to_rows):
    caches = [refs[a * pps:(a + 1) * pps] for a in range(n_arrays)]
    tails = refs[n_arrays * pps:n_arrays * (pps + 1)]
    outs = refs[n_arrays * (pps + 1):]
    p = pl.program_id(1)

    @pl.when(p < n_steps)
    def _():
        for c_refs, o_ref, rows in zip(caches, outs, to_rows):
            if rows:
                o_ref[0] = jnp.concatenate([c[0].T for c in c_refs], axis=0).astype(o_ref.dtype)
            else:
                o_ref[0] = jnp.concatenate([c[0] for c in c_refs], axis=1).astype(o_ref.dtype)

    @pl.when(p == n_steps)
    def _():
        for t_ref, o_ref in zip(tails, outs):
            o_ref[0] = t_ref[0].astype(o_ref.dtype)


def _gather_pages(page_table, caches, tails, dtypes, to_rows, pps):
    batch, n_pages = page_table.shape
    n, n_steps, span = len(caches), n_pages // pps, pps * PAGE_SIZE
    l_pad = (n_steps + 1) * span
    in_specs = [pl.BlockSpec((1,) + c.shape[1:],
                             lambda b, p, pt, i=i: (pt[b, jnp.minimum(p * pps + i, n_pages - 1)], 0, 0))
                for c in caches for i in range(pps)]
    in_specs += [pl.BlockSpec((1,) + t.shape[1:], lambda b, p, pt: (b, 0, 0)) for t in tails]
    out_specs, out_shape = [], []
    for c, dt, rows in zip(caches, dtypes, to_rows):
        if rows:
            out_specs.append(pl.BlockSpec((1, span, c.shape[1]), lambda b, p, pt: (b, p, 0)))
            out_shape.append(jax.ShapeDtypeStruct((batch, l_pad, c.shape[1]), dt))
        else:
            out_specs.append(pl.BlockSpec((1, c.shape[1], span), lambda b, p, pt: (b, 0, p)))
            out_shape.append(jax.ShapeDtypeStruct((batch, c.shape[1], l_pad), dt))
    operands = [c for c in caches for _ in range(pps)] + list(tails)
    return pl.pallas_call(
        functools.partial(_gather_kernel, n_steps=n_steps, n_arrays=n, pps=pps, to_rows=tuple(to_rows)),
        grid_spec=pltpu.PrefetchScalarGridSpec(num_scalar_prefetch=1, grid=(batch, n_steps + 1),
                                               in_specs=in_specs, out_specs=out_specs),
        out_shape=out_shape,
        compiler_params=_cparams("parallel", "arbitrary"),
        name="gather_pages",
    )(page_table, *operands)


def _outproj_kernel(oa_ref, ob_ref, ga_ref, gb_ref, x_ref, wpa_ref, wpb_ref, wo_ref, g_ref, wr_ref, br_ref,
                    x1_ref, h2_ref, gate_ref):
    pa = jnp.dot(oa_ref[...], wpa_ref[...], preferred_element_type=jnp.float32)
    pb = jnp.dot(ob_ref[...], wpb_ref[...], preferred_element_type=jnp.float32)
    merged = jax.nn.sigmoid(ga_ref[...]) * pa + jax.nn.sigmoid(gb_ref[...]) * pb
    x1 = x_ref[...] + jnp.dot(merged.astype(jnp.bfloat16), wo_ref[...], preferred_element_type=jnp.float32)
    x1_ref[...] = x1
    ms = jnp.mean(x1 * x1, axis=-1, keepdims=True)
    h2 = x1 * lax.rsqrt(ms + RMS_EPS) * g_ref[...]
    h2_ref[...] = h2.astype(h2_ref.dtype)

    hs, ws = _split3(h2), _split3(wr_ref[...])
    logits = br_ref[...]
    for a in range(3):
        for b in range(3 - a):
            logits = logits + jnp.dot(hs[a], ws[b], preferred_element_type=jnp.float32)

    tm = logits.shape[0]
    lane = lax.broadcasted_iota(jnp.int32, (tm, N_EXPERTS), 1).astype(jnp.float32)
    work, vals, hots = logits, [], []
    for _ in range(TOP_K):
        m = jnp.max(work, axis=-1, keepdims=True)
        first = jnp.min(jnp.where(work == m, lane, float(N_EXPERTS)), axis=-1, keepdims=True)
        hot = lane == first
        vals.append(m)
        hots.append(hot)
        work = jnp.where(hot, -jnp.inf, work)
    exps = [jnp.exp(v - vals[0]) for v in vals]
    den = exps[0] + exps[1] + exps[2] + exps[3]
    gate = jnp.zeros_like(logits)
    for hot, e in zip(hots, exps):
        gate = gate + jnp.where(hot, e / den, 0.0)
    gate_ref[...] = gate


def _outproj(oa, ob, zf, x_all, wpa, wpb, wo, g_ffn, w_router, b_router, tm):
    n = x_all.shape[0]
    row = lambda w: pl.BlockSpec((tm, w), lambda i: (i, 0))
    full = lambda a: pl.BlockSpec(a.shape, lambda i: (0,) * a.ndim)
    return pl.pallas_call(
        _outproj_kernel,
        grid=(n // tm,),
        in_specs=[row(oa.shape[1]), row(ob.shape[1]),
                  pl.BlockSpec((tm, D_MODEL), lambda i: (i, 0)), pl.BlockSpec((tm, D_MODEL), lambda i: (i, 1)),
                  row(D_MODEL), full(wpa), full(wpb), full(wo), full(g_ffn), full(w_router), full(b_router)],
        out_specs=[row(D_MODEL), row(D_MODEL), row(N_EXPERTS)],
        out_shape=[jax.ShapeDtypeStruct((n, D_MODEL), jnp.float32),
                   jax.ShapeDtypeStruct((n, D_MODEL), jnp.bfloat16),
                   jax.ShapeDtypeStruct((n, N_EXPERTS), jnp.float32)],
        compiler_params=_cparams("parallel"),
        name="outproj_router",
    )(oa, ob, zf, zf, x_all, wpa, wpb, wo, g_ffn, w_router, b_router)


def _moe_kernel(h_ref, gate_ref, x1_ref, wug_ref, bug_ref, wd_ref, bd_ref, gf_ref, y_ref, acc_ref, *, rows):
    e = pl.program_id(1)
    tt = h_ref.shape[0]

    @pl.when(e == 0)
    def _():
        acc_ref[...] = x1_ref[...]

    lane = lax.broadcasted_iota(jnp.int32, (rows, N_EXPERTS), 1)
    for r0 in range(0, tt, rows):
        g = jnp.sum(jnp.where(lane == e, gate_ref[r0:r0 + rows, :], 0.0), axis=-1, keepdims=True)
        ug = jnp.dot(h_ref[r0:r0 + rows, :], wug_ref[0], preferred_element_type=jnp.float32) + bug_ref[0]
        gt = jnp.minimum(ug[:, :D_FF], SWIGLU_LIMIT)
        up = jnp.clip(ug[:, D_FF:], -SWIGLU_LIMIT, SWIGLU_LIMIT)
        act = (up + 1.0) * gt * jax.nn.sigmoid(gt * SWIGLU_ALPHA)
        out = jnp.dot(act.astype(jnp.bfloat16), wd_ref[0], preferred_element_type=jnp.float32) + bd_ref[0]
        acc_ref[r0:r0 + rows, :] += g * out

    @pl.when(e == N_EXPERTS - 1)
    def _():
        x = acc_ref[...]
        ms = jnp.mean(x * x, axis=-1, keepdims=True)
        y_ref[...] = x * lax.rsqrt(ms + RMS_EPS) * gf_ref[...]


MOE_TOKEN_TILE_MAX = 1536


def _largest_tile(n, unit, cap):
    return max(t for t in range(unit, cap + 1, unit) if n % t == 0)


def _moe(h2, gate, x1, w_ug, b_ug, w_down, b_down, g_final, tt):
    n = h2.shape[0]
    rows = max(r for r in (256, 128, 64, 32, 16, 8) if tt % r == 0)
    tok = lambda w: pl.BlockSpec((tt, w), lambda i, e: (i, 0))
    exp = lambda a: pl.BlockSpec((1,) + a.shape[1:], lambda i, e: (e, 0, 0))
    return pl.pallas_call(
        functools.partial(_moe_kernel, rows=rows),
        grid=(n // tt, N_EXPERTS),
        in_specs=[tok(D_MODEL), tok(N_EXPERTS), tok(D_MODEL), exp(w_ug), exp(b_ug), exp(w_down), exp(b_down),
                  pl.BlockSpec((1, D_MODEL), lambda i, e: (0, 0))],
        out_specs=tok(D_MODEL),
        out_shape=jax.ShapeDtypeStruct((n, D_MODEL), jnp.float32),
        scratch_shapes=[pltpu.VMEM((tt, D_MODEL), jnp.float32)],
        compiler_params=_cparams("parallel", "arbitrary"),
        name="moe",
    )(h2, gate, x1, w_ug, b_ug, w_down, b_down, g_final)


def _zf_col(name):
    return (ZF_GATES + ZB[name][0]) // ZB[name][1]


def _zb_col(name):
    return ZB[name][0] // ZB[name][1]


def _layer(x_p, x_s, caches, page_table, w, *, tm=256, tq_prompt=128, tk_prompt=512, pps=8):
    (c_ak, c_av, c_aik, c_bck, c_bcv, c_bsk, c_bsv, s_wk, s_wv) = caches
    bp, s_len, _ = x_p.shape
    bs, t_new, _ = x_s.shape
    n_p, n_s_tok = bp * s_len, bs * t_new
    past = page_table.shape[1] * PAGE_SIZE
    l_s = past + t_new

    x_all = jnp.concatenate([x_p.reshape(n_p, D_MODEL), x_s.reshape(n_s_tok, D_MODEL)], axis=0)
    pos_table = jnp.concatenate([jnp.arange(s_len), jnp.tile(past + jnp.arange(t_new), bs)])
    zf, zb = _inproj(x_all, w["g_mix"].reshape(1, D_MODEL), _permute_w_in(w["w_in"]), _rope_table(pos_table),
                     s_len // tm, n_p // tm, tm)
    wk = _per_head_weights(w["w_cmp_k"])
    wv = _per_head_weights(w["w_cmp_v"])

    zf_p = zf[:n_p].reshape(bp, s_len, ZF_W)
    zb_p = zb[:n_p].reshape(bp, s_len, ZB_W)
    n_c_p = (s_len - CMP_LEN) // CMP_STRIDE + 1
    kc_p, vc_p = _compress((zf_p, _zf_col("kcb")), (zf_p, _zf_col("vcb")), wk, wv,
                           batch=bp, rows=s_len, nb=n_c_p + 1)
    oa_p = _dsa((zb_p, _zb_col("qa")), (zb_p, _zb_col("qi")), (zf_p, _zf_col("misc")),
                (zb_p, _zb_col("misc")), (zb_p, _zb_col("ka")), (zb_p, _zb_col("va")),
                batch=bp, t_q=s_len, tq=tq_prompt, l_pad=s_len, pos_base=0, k_top=min(IDX_TOPK_MAX, s_len // 4),
                fm=False, tk=tk_prompt)
    ob_p = _nsa((zb_p, _zb_col("qb")), (zf_p, _zf_col("misc")), kc_p, vc_p,
                (zb_p, _zb_col("ksb")), (zb_p, _zb_col("vsb")), (zb_p, _zb_col("kwb")), (zb_p, _zb_col("vwb")),
                batch=bp, t_q=s_len, tq=tq_prompt, l_pad=s_len, w_pad=s_len, pos_base=0,
                n_c=n_c_p, n_s=-(-s_len // SEL_BLOCK), win_base=0, fm=False, tk=tk_prompt)

    zf_s = zf[n_p:].reshape(bs, t_new, ZF_W)
    new = lambda name: zf_s[:, :, ZF_GATES + ZB[name][0]:ZF_GATES + ZB[name][0] + ZB[name][1]]
    bf, f32 = jnp.bfloat16, jnp.float32
    span = pps * PAGE_SIZE
    tk_s = span

    def fm_page(c):
        return jnp.transpose(c, (0, 2, 3, 1)).reshape(c.shape[0], KV_W, PAGE_SIZE)

    def fm_rows(a, pad_to):
        a = jnp.transpose(a, (0, 2, 1))
        return jnp.pad(a, ((0, 0), (0, 0), (0, pad_to - a.shape[-1])))

    row_tail = lambda name: jnp.pad(new(name), ((0, 0), (0, span - t_new), (0, 0)))
    (ak_f, av_f, aik_f, bck_f, bcv_f, bsk_f, bsv_f) = _gather_pages(
        page_table,
        [fm_page(c_ak), fm_page(c_av), jnp.transpose(c_aik, (0, 2, 1)), fm_page(c_bck), fm_page(c_bcv),
         fm_page(c_bsk), fm_page(c_bsv)],
        [fm_rows(new("ka"), span), fm_rows(new("va"), span), fm_rows(new("misc")[:, :, :IDX_DIM], span),
         row_tail("kcb"), row_tail("vcb"), fm_rows(new("ksb"), span), fm_rows(new("vsb"), span)],
        [bf, bf, bf, f32, f32, bf, bf],
        [False, False, False, True, True, False, False], pps)
    l_pad = past + span
    n_c_s = (l_s - CMP_LEN) // CMP_STRIDE + 1
    kc_s, vc_s = _compress((bck_f, 0), (bcv_f, 0), wk, wv, batch=bs, rows=l_pad, nb=n_c_s + 1)
    w_buf = s_wk.shape[1]
    w_pad = -(-(w_buf + t_new) // tk_s) * tk_s
    win = lambda state, name: jnp.concatenate(
        [jnp.transpose(state, (0, 2, 3, 1)).reshape(bs, KV_W, w_buf), fm_rows(new(name), w_pad - w_buf)], axis=-1)
    kw_all, vw_all = win(s_wk, "kwb"), win(s_wv, "vwb")
    oa_s = _dsa((zf_s, _zf_col("qa")), (zf_s, _zf_col("qi")), (zf_s, _zf_col("misc")),
                (aik_f, 0), (ak_f, 0), (av_f, 0), batch=bs, t_q=t_new, tq=t_new, l_pad=l_pad, pos_base=past,
                k_top=min(IDX_TOPK_MAX, l_s // 4), fm=True, tk=tk_s)
    ob_s = _nsa((zf_s, _zf_col("qb")), (zf_s, _zf_col("misc")), kc_s, vc_s, (bsk_f, 0), (bsv_f, 0),
                (kw_all.astype(bf), 0), (vw_all.astype(bf), 0),
                batch=bs, t_q=t_new, tq=t_new, l_pad=l_pad, w_pad=w_pad, pos_base=past,
                n_c=n_c_s, n_s=-(-l_s // SEL_BLOCK), win_base=past - w_buf, fm=True, tk=tk_s)

    oa = jnp.concatenate([oa_p.reshape(n_p, -1), oa_s.reshape(n_s_tok, -1)], axis=0)
    ob = jnp.concatenate([ob_p.reshape(n_p, -1), ob_s.reshape(n_s_tok, -1)], axis=0)
    x1, h2, gate = _outproj(oa, ob, zf, x_all, _pair_head_rows(w["w_proj_a"]).astype(bf), _pair_head_rows(w["w_proj_b"]).astype(bf),
                            w["w_out"].astype(bf), w["g_ffn"].reshape(1, D_MODEL), w["w_router"],
                            w["b_router"].reshape(1, N_EXPERTS), tm)
    y = _moe(h2, gate, x1, w["w_ug"].astype(bf), w["b_ug"].reshape(N_EXPERTS, 1, 2 * D_FF),
             w["w_down"].astype(bf), w["b_down"].reshape(N_EXPERTS, 1, D_MODEL), w["g_final"].reshape(1, D_MODEL),
             tt=_largest_tile(n_p + n_s_tok, tm, MOE_TOKEN_TILE_MAX))

    heads = lambda a: a.reshape(a.shape[0], a.shape[1], A_KV_HEADS, HEAD_DIM)
    col_p = lambda name: zf_p[:, :, ZF_GATES + ZB[name][0]:ZF_GATES + ZB[name][0] + ZB[name][1]]
    w_keep = min(WINDOW, s_len)
    st_p = (heads(col_p("ka")), heads(col_p("va")), col_p("misc")[:, :, :IDX_DIM], heads(col_p("kcb")),
            heads(col_p("vcb")), heads(col_p("ksb")), heads(col_p("vsb")),
            heads(col_p("kwb")[:, s_len - w_keep:]), heads(col_p("vwb")[:, s_len - w_keep:]))
    st_s = (heads(new("ka")), heads(new("va")), new("misc")[:, :, :IDX_DIM], heads(new("kcb")), heads(new("vcb")),
            heads(new("ksb")), heads(new("vsb")),
            heads(jnp.transpose(kw_all[..., t_new:t_new + w_buf], (0, 2, 1))),
            heads(jnp.transpose(vw_all[..., t_new:t_new + w_buf], (0, 2, 1))))
    return y[:n_p].reshape(bp, s_len, D_MODEL), y[n_p:].reshape(bs, t_new, D_MODEL), st_p, st_s


def kernel(x_prompt, x_sample, cache_a_k, cache_a_v, cache_a_idx_k, cache_b_cmp_k, cache_b_cmp_v, cache_b_slc_k,
           cache_b_slc_v, state_b_win_k, state_b_win_v, page_table, g_mix, w_in, w_cmp_k, w_cmp_v, w_proj_a,
           w_proj_b, w_out, g_ffn, w_router, b_router, w_ug, b_ug, w_down, b_down, g_final):
    depth = g_mix.shape[0]
    assert depth == 1, "the final rmsnorm is fused into the last layer's expert kernel; one layer is supported"
    caches = (cache_a_k[0], cache_a_v[0], cache_a_idx_k[0], cache_b_cmp_k[0], cache_b_cmp_v[0], cache_b_slc_k[0],
              cache_b_slc_v[0], state_b_win_k[0], state_b_win_v[0])
    w = dict(g_mix=g_mix[0], w_in=w_in[0], w_cmp_k=w_cmp_k[0], w_cmp_v=w_cmp_v[0], w_proj_a=w_proj_a[0],
             w_proj_b=w_proj_b[0], w_out=w_out[0], g_ffn=g_ffn[0], w_router=w_router[0], b_router=b_router[0],
             w_ug=w_ug[0], b_ug=b_ug[0], w_down=w_down[0], b_down=b_down[0], g_final=g_final)
    y_p, y_s, st_p, st_s = _layer(x_prompt, x_sample, caches, page_table, w)
    return (y_p, y_s) + tuple(s[None] for s in st_p) + tuple(s[None] for s in st_s)
```

```python
import functools

import numpy as np
import jax
import jax.numpy as jnp
from jax import lax
from jax.experimental import pallas as pl
from jax.experimental.pallas import tpu as pltpu

D_MODEL = 1024
HEAD_DIM = 64
PAGE_SIZE = 128
A_HEADS = 8
A_KV_HEADS = 2
IDX_HEADS = 4
IDX_DIM = 64
IDX_TOPK_MAX = 256
B_HEADS = 8
B_KV_HEADS = 2
NSA_BRANCHES = 3
CMP_LEN = 32
CMP_STRIDE = 16
SEL_BLOCK = 64
SEL_TOPN = 16
WINDOW = 512
N_EXPERTS = 32
TOP_K = 4
D_FF = 1024
SWIGLU_LIMIT = 7.0
SWIGLU_ALPHA = 1.702
ROPE_THETA = 10000.0
RMS_EPS = 1e-5
PROJ_WIDTHS = (A_HEADS * HEAD_DIM, A_KV_HEADS * HEAD_DIM, A_KV_HEADS * HEAD_DIM, IDX_HEADS * IDX_DIM, IDX_DIM,
               IDX_HEADS, B_HEADS * HEAD_DIM, B_KV_HEADS * HEAD_DIM, B_KV_HEADS * HEAD_DIM, B_KV_HEADS * HEAD_DIM,
               B_KV_HEADS * HEAD_DIM, B_KV_HEADS * HEAD_DIM, B_KV_HEADS * HEAD_DIM, B_HEADS * NSA_BRANCHES,
               D_MODEL, D_MODEL)

LANES = 128
GQA = A_HEADS // A_KV_HEADS
KV_W = A_KV_HEADS * HEAD_DIM
VMEM_LIMIT = 48 * 1024 * 1024

ZB = dict(qa=(0, 512), qb=(512, 512), qi=(1024, 256), ka=(1280, 128), va=(1408, 128), misc=(1536, 128),
          kcb=(1664, 128), vcb=(1792, 128), ksb=(1920, 128), vsb=(2048, 128), kwb=(2176, 128), vwb=(2304, 128))
ZB_W = 2432
ZF_GATES = 2 * D_MODEL
ZF_W = ZF_GATES + ZB_W
MISC_WI = IDX_DIM
MISC_GB = IDX_DIM + IDX_HEADS
ROPE_GROUPS = ("qa", "qb", "qi", "ka", "kcb", "ksb", "kwb")

NEG = -1e30
Q_SCALE = HEAD_DIM ** -0.5 * float(np.log2(np.e))
INT_MIN = -2 ** 31
KEY_PINF = 0x7F800000
KEY_NINF = -0x7F800001

_NT = (((1,), (1,)), ((), ()))


def _cparams(*sem):
    return pltpu.CompilerParams(dimension_semantics=sem, vmem_limit_bytes=VMEM_LIMIT)


def _inproj_kernel(x_ref, g_ref, w_ref, cs_ref, zf_ref, zb_ref):
    x = x_ref[...]
    ms = jnp.mean(x * x, axis=-1, keepdims=True)
    h = (x * lax.rsqrt(ms + RMS_EPS) * g_ref[...]).astype(jnp.bfloat16)
    tm = x.shape[0]
    lane = lax.broadcasted_iota(jnp.int32, (tm, LANES), 1)
    first_half = (lane % HEAD_DIM) < (HEAD_DIM // 2)

    def rope(z, c, s):
        partner = jnp.where(first_half, pltpu.roll(z, LANES - HEAD_DIM // 2, 1), pltpu.roll(z, HEAD_DIM // 2, 1))
        return z * c + partner * s

    cos, sin = cs_ref[:, 0:LANES], cs_ref[:, LANES:2 * LANES]
    cos_m, sin_m = cs_ref[:, 2 * LANES:3 * LANES], cs_ref[:, 3 * LANES:4 * LANES]

    for c0 in (0, D_MODEL):
        zf_ref[:, c0:c0 + D_MODEL] = jnp.dot(h, w_ref[:, c0:c0 + D_MODEL], preferred_element_type=jnp.float32)
    for name, (off, width) in ZB.items():
        z = jnp.dot(h, w_ref[:, ZF_GATES + off:ZF_GATES + off + width], preferred_element_type=jnp.float32)
        if name in ROPE_GROUPS:
            z = jnp.concatenate([rope(z[:, c:c + LANES], cos, sin) for c in range(0, width, LANES)], axis=1)
        elif name == "misc":
            z = rope(z, cos_m, sin_m)
        zf_ref[:, ZF_GATES + off:ZF_GATES + off + width] = z
        zb_ref[:, off:off + width] = z.astype(jnp.bfloat16)


def _inproj(x_all, g_mix, w_perm, cs_table, tiles_per_seq, n_prompt_tiles, tm):
    n = x_all.shape[0]

    def cs_map(i):
        return (jnp.where(i < n_prompt_tiles, i % tiles_per_seq, tiles_per_seq + i - n_prompt_tiles), 0)

    return pl.pallas_call(
        _inproj_kernel,
        grid=(n // tm,),
        in_specs=[pl.BlockSpec((tm, D_MODEL), lambda i: (i, 0)),
                  pl.BlockSpec((1, D_MODEL), lambda i: (0, 0)),
                  pl.BlockSpec((D_MODEL, ZF_W), lambda i: (0, 0)),
                  pl.BlockSpec((tm, 4 * LANES), cs_map)],
        out_specs=[pl.BlockSpec((tm, ZF_W), lambda i: (i, 0)),
                   pl.BlockSpec((tm, ZB_W), lambda i: (i, 0))],
        out_shape=[jax.ShapeDtypeStruct((n, ZF_W), jnp.float32),
                   jax.ShapeDtypeStruct((n, ZB_W), jnp.bfloat16)],
        compiler_params=_cparams("parallel"),
        name="inproj",
    )(x_all, g_mix, w_perm, cs_table)


HEAD_PAIR_ORDER = tuple(h for g in range(GQA) for h in (g, GQA + g))


def _pair_head_cols(w):
    return w.reshape(w.shape[0], A_HEADS, HEAD_DIM)[:, HEAD_PAIR_ORDER, :].reshape(w.shape)


def _pair_head_rows(w):
    return w.reshape(A_HEADS, HEAD_DIM, w.shape[1])[HEAD_PAIR_ORDER, :, :].reshape(w.shape)


def _permute_w_in(w_in):
    offs = np.cumsum((0,) + PROJ_WIDTHS)
    (qa, ka, va, qi, ki, wi, qb, kcb, vcb, ksb, vsb, kwb, vwb, gb, gate_a, gate_b) = [
        w_in[:, offs[i]:offs[i + 1]] for i in range(len(PROJ_WIDTHS))]
    pad = jnp.zeros((w_in.shape[0], LANES - IDX_DIM - IDX_HEADS - B_HEADS * NSA_BRANCHES), w_in.dtype)
    misc = jnp.concatenate([ki, wi, gb, pad], axis=1)
    qa, qb = _pair_head_cols(qa), _pair_head_cols(qb)
    w = jnp.concatenate([gate_a, gate_b, qa, qb, qi, ka, va, misc, kcb, vcb, ksb, vsb, kwb, vwb], axis=1)
    return w.astype(jnp.bfloat16)


def _rope_table(pos):
    half = HEAD_DIM // 2
    inv = jnp.power(ROPE_THETA, -jnp.arange(half, dtype=jnp.float32) / half)
    ang = pos.astype(jnp.float32)[:, None] * inv[None, :]
    c, s = jnp.cos(ang), jnp.sin(ang)
    one, zero = jnp.ones_like(c), jnp.zeros_like(c)
    return jnp.concatenate([c, c, c, c, -s, s, -s, s, c, c, one, one, -s, s, zero, zero], axis=1)


def _stack_queries(q):
    tq = q.shape[0]
    low = lax.broadcasted_iota(jnp.int32, (tq, LANES), 1) < HEAD_DIM
    chunks = [q[:, g * LANES:(g + 1) * LANES] for g in range(GQA)]
    zero = jnp.zeros_like(chunks[0])
    return jnp.concatenate([jnp.where(low, c, zero) for c in chunks] + [jnp.where(low, zero, c) for c in chunks],
                           axis=0).astype(jnp.bfloat16)


def _pair_heads(o, tq):
    low = lax.broadcasted_iota(jnp.int32, (tq, LANES), 1) < HEAD_DIM
    return [jnp.where(low, o[g * tq:(g + 1) * tq], o[(GQA + g) * tq:(GQA + g + 1) * tq]) for g in range(GQA)]


def _flash(q_all, k_ref, v_ref, lo, hi, bias_fn, carry0, m_scr, l_scr, acc_scr, tk, fm):
    m_scr[...] = jnp.full(m_scr.shape, NEG, jnp.float32)
    l_scr[...] = jnp.zeros(l_scr.shape, jnp.float32)
    acc_scr[...] = jnp.zeros(acc_scr.shape, jnp.float32)
    n_chunks = tk // LANES

    def body(ti, carry):
        k0 = pl.multiple_of(ti * tk, tk)
        biases, carry = bias_fn(k0, carry)
        bias = jnp.concatenate([biases[0]] * GQA + [biases[1]] * GQA, axis=0)
        if fm:
            s = jnp.dot(q_all, k_ref[0, :, pl.ds(k0, tk)], preferred_element_type=jnp.float32)
        else:
            s = lax.dot_general(q_all, k_ref[0, pl.ds(k0, tk), :], _NT, preferred_element_type=jnp.float32)
        s = s + bias
        chunks = [s[:, c * LANES:(c + 1) * LANES] for c in range(n_chunks)]
        cm = chunks[0]
        for c in chunks[1:]:
            cm = jnp.maximum(cm, c)
        m_old = m_scr[...]
        m_new = jnp.maximum(m_old, jnp.max(cm, axis=-1, keepdims=True))
        alpha = jnp.exp2(m_old - m_new)
        ps = [jnp.exp2(c - m_new) for c in chunks]
        psum = ps[0]
        for p in ps[1:]:
            psum = psum + p
        l_scr[...] = alpha * l_scr[...] + jnp.sum(psum, axis=-1, keepdims=True)
        p16 = jnp.concatenate(ps, axis=1).astype(jnp.bfloat16)
        if fm:
            pv = lax.dot_general(p16, v_ref[0, :, pl.ds(k0, tk)], _NT, preferred_element_type=jnp.float32)
        else:
            pv = jnp.dot(p16, v_ref[0, pl.ds(k0, tk), :], preferred_element_type=jnp.float32)
        acc_scr[...] = alpha * acc_scr[...] + pv
        m_scr[...] = m_new
        return carry

    lax.fori_loop(lo, hi, body, carry0)
    return jnp.where(m_scr[...] > 0.5 * NEG, acc_scr[...] / l_scr[...], 0.0)


def _flash_scratch(tq):
    return [pltpu.VMEM((2 * GQA * tq, LANES), jnp.float32) for _ in range(3)]


def _kv_spec(src, rows, fm):
    if fm:
        return pl.BlockSpec((1,) + src[0].shape[1:], lambda b, i: (b, 0, 0))
    return pl.BlockSpec((1, rows, KV_W), lambda b, i, c=src[1]: (b, 0, c))


def _dsa_kernel(q_ref, qi_ref, misc_ref, ki_ref, k_ref, v_ref, o_ref, key_scr, m_scr, l_scr, acc_scr,
                *, tq, tk, n_key_tiles, pos_base, k_top, fm):
    qb = pl.program_id(1)
    q_lo = pos_base + qb * tq
    nk = jnp.minimum((q_lo + tq + tk - 1) // tk, n_key_tiles)
    q_pos = q_lo + lax.broadcasted_iota(jnp.int32, (tq, 1), 0)
    lane_k = lax.broadcasted_iota(jnp.int32, (tq, tk), 1)
    n_chunks = tk // LANES

    qi = qi_ref[0].astype(jnp.float32) * IDX_DIM ** -0.5
    if fm:
        qh = [qi[:, h * IDX_DIM:(h + 1) * IDX_DIM].astype(jnp.bfloat16) for h in range(IDX_HEADS)]
    else:
        low = lax.broadcasted_iota(jnp.int32, (tq, LANES), 1) < IDX_DIM
        qh = []
        for h in range(IDX_HEADS):
            c = qi[:, (h // 2) * LANES:(h // 2 + 1) * LANES]
            if h % 2:
                c = pltpu.roll(c, IDX_DIM, 1)
            qh.append(jnp.where(low, c, 0.0).astype(jnp.bfloat16))
    wi = misc_ref[0][:, MISC_WI:MISC_WI + IDX_HEADS] * IDX_HEADS ** -0.5
    wi_b = [jnp.broadcast_to(wi[:, h:h + 1], (tq, tk)) for h in range(IDX_HEADS)]

    def score_tile(ti, _):
        k0 = pl.multiple_of(ti * tk, tk)
        acc = jnp.zeros((tq, tk), jnp.float32)
        for h in range(IDX_HEADS):
            if fm:
                s = jnp.dot(qh[h], ki_ref[0, :, pl.ds(k0, tk)], preferred_element_type=jnp.float32)
            else:
                s = lax.dot_general(qh[h], ki_ref[0, pl.ds(k0, tk), :], _NT, preferred_element_type=jnp.float32)
            acc = acc + jnp.maximum(s, 0.0) * wi_b[h]
        bits = pltpu.bitcast(acc, jnp.int32)
        key = jnp.where(bits < 0, bits ^ 0x7FFFFFFF, bits)
        key_scr[:, pl.ds(k0, tk)] = jnp.where(k0 + lane_k <= q_pos, key, INT_MIN)
        return 0

    lax.fori_loop(0, nk, score_tile, 0)

    def count_ge(cand):
        cand_b = jnp.broadcast_to(cand, (tq, LANES))

        def body(ti, c):
            k0 = pl.multiple_of(ti * tk, tk)
            for j in range(n_chunks):
                c = c + jnp.where(key_scr[:, pl.ds(k0 + j * LANES, LANES)] >= cand_b, 1.0, 0.0)
            return c
        c = lax.fori_loop(0, nk, body, jnp.zeros((tq, LANES), jnp.float32))
        return jnp.sum(c, axis=-1, keepdims=True)

    def search(i, ans):
        cand = ans + jnp.left_shift(jnp.int32(1), 31 - i)
        return jnp.where(count_ge(cand) >= float(k_top), cand, ans)

    thr = lax.fori_loop(0, 32, search, jnp.full((tq, 1), INT_MIN, jnp.int32))
    need = float(k_top) - count_ge(jnp.where(thr == 2 ** 31 - 1, thr, thr + 1))

    row_i = lax.broadcasted_iota(jnp.int32, (tk, tk), 0)
    col_i = lax.broadcasted_iota(jnp.int32, (tk, tk), 1)
    before = jnp.where(row_i < col_i, 1.0, 0.0).astype(jnp.bfloat16)

    def bias_fn(k0, ties_seen):
        key = key_scr[:, pl.ds(k0, tk)]
        eq = jnp.where(key == thr, 1.0, 0.0)
        rank_eq = jnp.dot(eq.astype(jnp.bfloat16), before, preferred_element_type=jnp.float32) + ties_seen
        take = (key > thr) | ((key == thr) & (rank_eq < need))
        take = take & (key > KEY_NINF) & (key < KEY_PINF)
        bias = jnp.where(take, 0.0, NEG)
        return [bias, bias], ties_seen + jnp.sum(eq, axis=-1, keepdims=True)

    q_all = _stack_queries(q_ref[0].astype(jnp.float32) * Q_SCALE)
    o = _flash(q_all, k_ref, v_ref, 0, nk, bias_fn, jnp.zeros((tq, 1), jnp.float32), m_scr, l_scr, acc_scr, tk, fm)
    o_ref[0] = jnp.concatenate(_pair_heads(o, tq), axis=1).astype(o_ref.dtype)


def _dsa(q_src, qi_src, misc_src, ki_src, k_src, v_src, *, batch, t_q, tq, l_pad, pos_base, k_top, fm, tk):
    def qspec(src, width):
        return pl.BlockSpec((1, tq, width), lambda b, i, c=src[1]: (b, i, c))

    kspec = lambda src: _kv_spec(src, l_pad, fm)
    kern = functools.partial(_dsa_kernel, tq=tq, tk=tk, n_key_tiles=l_pad // tk, pos_base=pos_base, k_top=k_top,
                             fm=fm)
    return pl.pallas_call(
        kern,
        grid=(batch, t_q // tq),
        in_specs=[qspec(q_src, A_HEADS * HEAD_DIM), qspec(qi_src, IDX_HEADS * IDX_DIM), qspec(misc_src, LANES),
                  kspec(ki_src), kspec(k_src), kspec(v_src)],
        out_specs=pl.BlockSpec((1, tq, A_HEADS * HEAD_DIM), lambda b, i: (b, i, 0)),
        out_shape=jax.ShapeDtypeStruct((batch, t_q, A_HEADS * HEAD_DIM), jnp.bfloat16),
        scratch_shapes=[pltpu.VMEM((tq, l_pad), jnp.int32)] + _flash_scratch(tq),
        compiler_params=_cparams("parallel", "arbitrary"),
        name="dsa",
    )(q_src[0], qi_src[0], misc_src[0], ki_src[0], k_src[0], v_src[0])


def _compress_kernel(xk_ref, xv_ref, wk_ref, wv_ref, kc_ref, vc_ref, *, nb):
    row = lax.broadcasted_iota(jnp.int32, (nb, KV_W), 0)
    for x_ref, w_ref, o_ref in ((xk_ref, wk_ref, kc_ref), (xv_ref, wv_ref, vc_ref)):
        first = jnp.zeros((nb, KV_W), jnp.float32)
        second = jnp.zeros((nb, KV_W), jnp.float32)
        for j in range(CMP_STRIDE):
            rows = x_ref[0, pl.ds(j, nb, stride=CMP_STRIDE), :].astype(jnp.bfloat16)
            first = first + jnp.dot(rows, w_ref[j], preferred_element_type=jnp.float32)
            second = second + jnp.dot(rows, w_ref[CMP_STRIDE + j], preferred_element_type=jnp.float32)
        tok = first + pltpu.roll(second, nb - 1, 0)
        o_ref[0] = jnp.where(row < nb - 1, tok, 0.0).astype(o_ref.dtype)


def _compress(xk_src, xv_src, wk, wv, *, batch, rows, nb):
    def xspec(src):
        return pl.BlockSpec((1, rows, KV_W), lambda b, c=src[1]: (b, 0, c))

    wspec = pl.BlockSpec((CMP_LEN, KV_W, KV_W), lambda b: (0, 0, 0))
    ospec = pl.BlockSpec((1, nb, KV_W), lambda b: (b, 0, 0))
    oshape = jax.ShapeDtypeStruct((batch, nb, KV_W), jnp.bfloat16)
    return pl.pallas_call(
        functools.partial(_compress_kernel, nb=nb),
        grid=(batch,),
        in_specs=[xspec(xk_src), xspec(xv_src), wspec, wspec],
        out_specs=[ospec, ospec],
        out_shape=[oshape, oshape],
        compiler_params=_cparams("parallel"),
        name="compress",
    )(xk_src[0], xv_src[0], wk, wv)


def _per_head_weights(w_c):
    z = jnp.zeros_like(w_c)
    return jnp.concatenate([jnp.concatenate([w_c, z], axis=2), jnp.concatenate([z, w_c], axis=2)],
                           axis=1).astype(jnp.bfloat16)


def _split3(x):
    hi = x.astype(jnp.bfloat16)
    r1 = x - hi.astype(jnp.float32)
    mid = r1.astype(jnp.bfloat16)
    lo = (r1 - mid.astype(jnp.float32)).astype(jnp.bfloat16)
    return hi, mid, lo


def _nsa_kernel(q_ref, misc_ref, kc_ref, vc_ref, ks_ref, vs_ref, kw_ref, vw_ref, o_ref, m_scr, l_scr, acc_scr,
                *, tq, tk, n_key_tiles, pos_base, n_c, n_s, win_base, n_win_tiles, fm):
    qb = pl.program_id(1)
    q_lo = pos_base + qb * tq
    nk = jnp.minimum((q_lo + tq + tk - 1) // tk, n_key_tiles)
    q_pos = q_lo + lax.broadcasted_iota(jnp.int32, (tq, 1), 0)
    ncp = kc_ref.shape[1]
    n_sel = min(SEL_TOPN, n_s)
    paired = n_s <= HEAD_DIM
    bw = HEAD_DIM if paired else -(-n_s // LANES) * LANES
    vw = LANES if paired else bw

    q_all = _stack_queries(q_ref[0].astype(jnp.float32) * Q_SCALE)

    n_i = lax.broadcasted_iota(jnp.int32, (tq, ncp), 1)
    c_ok = (n_i * CMP_STRIDE + CMP_LEN - 1 <= q_pos) & (n_i < n_c)
    c_mask = jnp.concatenate([jnp.where(c_ok, 1.0, 0.0)] * (2 * GQA), axis=0) > 0.5
    s = lax.dot_general(q_all, kc_ref[0], _NT, preferred_element_type=jnp.float32)
    s = jnp.where(c_mask, s, -jnp.inf)
    m = jnp.max(s, axis=-1, keepdims=True)
    m = jnp.where(m > -jnp.inf, m, 0.0)
    e = jnp.where(c_mask, jnp.exp2(s - m), 0.0)
    den = jnp.sum(e, axis=-1, keepdims=True)
    pc = e / jnp.where(den > 0.0, den, 1.0)
    o_cmp = jnp.dot(pc.astype(jnp.bfloat16), vc_ref[0], preferred_element_type=jnp.float32)

    pg = []
    for j in range(B_KV_HEADS):
        t = pc[GQA * j * tq:(GQA * j + 1) * tq]
        for g in range(1, GQA):
            t = t + pc[(GQA * j + g) * tq:(GQA * j + g + 1) * tq]
        pg.append(t)
    sel = []
    if paired:
        nb8 = -(-n_s // 8) * 8
        ov_s = lax.broadcasted_iota(jnp.int32, (nb8, ncp), 0) * SEL_BLOCK
        ov_n = lax.broadcasted_iota(jnp.int32, (nb8, ncp), 1) * CMP_STRIDE
        overlap_t = jnp.where((ov_n < ov_s + SEL_BLOCK) & (ov_n + CMP_LEN > ov_s), 1.0, 0.0).astype(jnp.bfloat16)
        blk_t = lax.broadcasted_iota(jnp.int32, (nb8, tq), 0)
        pos_t = q_lo + lax.broadcasted_iota(jnp.int32, (nb8, tq), 1)
        forced_t = ((blk_t == 0) | (blk_t == pos_t // SEL_BLOCK)) & (blk_t < n_s)
        visible_t = (blk_t * SEL_BLOCK <= pos_t) & (blk_t < n_s)
        lane_v = lax.broadcasted_iota(jnp.int32, (tq, LANES), 1)
        for j in range(B_KV_HEADS):
            imp = sum(lax.dot_general(overlap_t, t, _NT, preferred_element_type=jnp.float32) for t in _split3(pg[j]))
            v = jnp.where(forced_t, jnp.inf, jnp.where(visible_t, imp, -jnp.inf))
            rank = jnp.zeros((nb8, tq), jnp.float32)
            for s2 in range(n_s):
                row = jnp.broadcast_to(v[s2:s2 + 1, :], (nb8, tq))
                rank = rank + jnp.where((row > v) | ((row == v) & (s2 < blk_t)), 1.0, 0.0)
            chosen = jnp.where((rank < float(n_sel)) & (v > -jnp.inf), 1.0, 0.0).T
            chosen = jnp.concatenate([chosen, jnp.zeros((tq, bw - nb8), jnp.float32)], axis=1) if bw > nb8 else chosen
            zero = jnp.zeros_like(chosen)
            sel.append(jnp.concatenate([chosen, zero] if j == 0 else [zero, chosen], axis=1))
    else:
        ov_n = lax.broadcasted_iota(jnp.int32, (ncp, vw), 0) * CMP_STRIDE
        ov_s = lax.broadcasted_iota(jnp.int32, (ncp, vw), 1) * SEL_BLOCK
        overlap = jnp.where((ov_n < ov_s + SEL_BLOCK) & (ov_n + CMP_LEN > ov_s), 1.0, 0.0).astype(jnp.bfloat16)
        blk = lax.broadcasted_iota(jnp.int32, (tq, vw), 1)
        forced = ((blk == 0) | (blk == q_pos // SEL_BLOCK)) & (blk < n_s)
        visible = (blk * SEL_BLOCK <= q_pos) & (blk < n_s)
        for j in range(B_KV_HEADS):
            imp = sum(jnp.dot(t, overlap, preferred_element_type=jnp.float32) for t in _split3(pg[j]))
            v = jnp.where(forced, jnp.inf, jnp.where(visible, imp, -jnp.inf))
            rank = jnp.zeros((tq, vw), jnp.float32)
            for s2 in range(n_s):
                col = jnp.broadcast_to(v[:, s2:s2 + 1], (tq, vw))
                rank = rank + jnp.where((col > v) | ((col == v) & (s2 < blk)), 1.0, 0.0)
            sel.append(jnp.where((rank < float(n_sel)) & (v > -jnp.inf), 1.0, 0.0))
    sel = [t.astype(jnp.bfloat16) for t in sel]

    lane_k = lax.broadcasted_iota(jnp.int32, (tq, tk), 1)
    e_blk = lax.broadcasted_iota(jnp.int32, (vw, tk), 0) % bw
    e_key = lax.broadcasted_iota(jnp.int32, (vw, tk), 1)

    def slc_bias(k0, carry):
        expand = jnp.where(e_blk == (k0 + e_key) // SEL_BLOCK, 1.0, 0.0).astype(jnp.bfloat16)
        causal = k0 + lane_k <= q_pos
        return [jnp.where(causal & (jnp.dot(sj, expand, preferred_element_type=jnp.float32) > 0.5), 0.0, NEG)
                for sj in sel], carry

    o_slc = _flash(q_all, ks_ref, vs_ref, 0, nk, slc_bias, 0, m_scr, l_scr, acc_scr, tk, fm)

    def win_bias(k0, carry):
        k_pos = win_base + k0 + lane_k
        dist = q_pos - k_pos
        bias = jnp.where((dist >= 0) & (dist < WINDOW) & (k_pos >= 0), 0.0, NEG)
        return [bias, bias], carry

    w_lo = jnp.maximum(q_lo - (WINDOW - 1) - win_base, 0) // tk
    w_hi = jnp.minimum((q_lo + tq - win_base + tk - 1) // tk, n_win_tiles)
    o_win = _flash(q_all, kw_ref, vw_ref, w_lo, w_hi, win_bias, 0, m_scr, l_scr, acc_scr, tk, fm)

    gate = jax.nn.sigmoid(misc_ref[0][:, MISC_GB:MISC_GB + B_HEADS * NSA_BRANCHES])
    low = lax.broadcasted_iota(jnp.int32, (tq, LANES), 1) < HEAD_DIM
    branches = [_pair_heads(o, tq) for o in (o_cmp, o_slc, o_win)]
    chunks = []
    for g in range(GQA):
        acc = jnp.zeros((tq, LANES), jnp.float32)
        for r in range(NSA_BRANCHES):
            c0, c1 = g * NSA_BRANCHES + r, (GQA + g) * NSA_BRANCHES + r
            gr = jnp.where(low, jnp.broadcast_to(gate[:, c0:c0 + 1], (tq, LANES)),
                           jnp.broadcast_to(gate[:, c1:c1 + 1], (tq, LANES)))
            acc = acc + gr * branches[r][g]
        chunks.append(acc)
    o_ref[0] = jnp.concatenate(chunks, axis=1).astype(o_ref.dtype)


def _nsa(q_src, misc_src, kc, vc, ks_src, vs_src, kw_src, vw_src, *, batch, t_q, tq, l_pad, w_pad, pos_base,
         n_c, n_s, win_base, fm, tk):
    def qspec(src, width):
        return pl.BlockSpec((1, tq, width), lambda b, i, c=src[1]: (b, i, c))

    kspec = lambda src, rows: _kv_spec(src, rows, fm)
    cspec = pl.BlockSpec((1,) + kc.shape[1:], lambda b, i: (b, 0, 0))
    kern = functools.partial(_nsa_kernel, tq=tq, tk=tk, n_key_tiles=l_pad // tk, pos_base=pos_base, n_c=n_c,
                             n_s=n_s, win_base=win_base, n_win_tiles=w_pad // tk, fm=fm)
    return pl.pallas_call(
        kern,
        grid=(batch, t_q // tq),
        in_specs=[qspec(q_src, B_HEADS * HEAD_DIM), qspec(misc_src, LANES),
                  cspec, cspec, kspec(ks_src, l_pad), kspec(vs_src, l_pad),
                  kspec(kw_src, w_pad), kspec(vw_src, w_pad)],
        out_specs=pl.BlockSpec((1, tq, B_HEADS * HEAD_DIM), lambda b, i: (b, i, 0)),
        out_shape=jax.ShapeDtypeStruct((batch, t_q, B_HEADS * HEAD_DIM), jnp.bfloat16),
        scratch_shapes=_flash_scratch(tq),
        compiler_params=_cparams("parallel", "arbitrary"),
        name="nsa",
    )(q_src[0], misc_src[0], kc, vc, ks_src[0], vs_src[0], kw_src[0], vw_src[0])


def _gather_kernel(pt_ref, *refs, n_steps, n_arrays, pps, to_rows):
    caches = [refs[a * pps:(a + 1) * pps] for a in range(n_arrays)]
    tails = refs[n_arrays * pps:n_arrays * (pps + 1)]
    outs = refs[n_arrays * (pps + 1):]
    p = pl.program_id(1)

    @pl.when(p < n_steps)
    def _():
        for c_refs, o_ref, rows in zip(caches, outs, to_rows):
            if rows:
                o_ref[0] = jnp.concatenate([c[0].T for c in c_refs], axis=0).astype(o_ref.dtype)
            else:
                o_ref[0] = jnp.concatenate([c[0] for c in c_refs], axis=1).astype(o_ref.dtype)

    @pl.when(p == n_steps)
    def _():
        for t_ref, o_ref in zip(tails, outs):
            o_ref[0] = t_ref[0].astype(o_ref.dtype)


def _gather_pages(page_table, caches, tails, dtypes, to_rows, pps):
    batch, n_pages = page_table.shape
    n, n_steps, span = len(caches), n_pages // pps, pps * PAGE_SIZE
    l_pad = (n_steps + 1) * span
    in_specs = [pl.BlockSpec((1,) + c.shape[1:],
                             lambda b, p, pt, i=i: (pt[b, jnp.minimum(p * pps + i, n_pages - 1)], 0, 0))
                for c in caches for i in range(pps)]
    in_specs += [pl.BlockSpec((1,) + t.shape[1:], lambda b, p, pt: (b, 0, 0)) for t in tails]
    out_specs, out_shape = [], []
    for c, dt, rows in zip(caches, dtypes, to_rows):
        if rows:
            out_specs.append(pl.BlockSpec((1, span, c.shape[1]), lambda b, p, pt: (b, p, 0)))
            out_shape.append(jax.ShapeDtypeStruct((batch, l_pad, c.shape[1]), dt))
        else:
            out_specs.append(pl.BlockSpec((1, c.shape[1], span), lambda b, p, pt: (b, 0, p)))
            out_shape.append(jax.ShapeDtypeStruct((batch, c.shape[1], l_pad), dt))
    operands = [c for c in caches for _ in range(pps)] + list(tails)
    return pl.pallas_call(
        functools.partial(_gather_kernel, n_steps=n_steps, n_arrays=n, pps=pps, to_rows=tuple(to_rows)),
        grid_spec=pltpu.PrefetchScalarGridSpec(num_scalar_prefetch=1, grid=(batch, n_steps + 1),
                                               in_specs=in_specs, out_specs=out_specs),
        out_shape=out_shape,
        compiler_params=_cparams("parallel", "arbitrary"),
        name="gather_pages",
    )(page_table, *operands)


ROW_TILES = D_MODEL // LANES


def _rows_to_tiles(ref, x):
    for c in range(ROW_TILES):
        ref[:, c, :] = x[:, c * LANES:(c + 1) * LANES]


def _tiles_to_rows(ref):
    return jnp.concatenate([ref[:, c, :] for c in range(ROW_TILES)], axis=1)


def _outproj_kernel(oa_ref, ob_ref, ga_ref, gb_ref, x_ref, wpa_ref, wpb_ref, wo_ref, g_ref, wr_ref, br_ref,
                    x1_ref, h2_ref, e_ref, r_ref, w_ref, cnt_ref, run_ref):
    @pl.when(pl.program_id(0) == 0)
    def _():
        run_ref[...] = jnp.zeros(run_ref.shape, jnp.float32)

    pa = jnp.dot(oa_ref[...], wpa_ref[...], preferred_element_type=jnp.float32)
    pb = jnp.dot(ob_ref[...], wpb_ref[...], preferred_element_type=jnp.float32)
    merged = jax.nn.sigmoid(ga_ref[...]) * pa + jax.nn.sigmoid(gb_ref[...]) * pb
    x1 = x_ref[...] + jnp.dot(merged.astype(jnp.bfloat16), wo_ref[...], preferred_element_type=jnp.float32)
    _rows_to_tiles(x1_ref, x1)
    ms = jnp.mean(x1 * x1, axis=-1, keepdims=True)
    h2 = x1 * lax.rsqrt(ms + RMS_EPS) * g_ref[...]
    _rows_to_tiles(h2_ref, h2)

    hs, ws = _split3(h2), _split3(wr_ref[...])
    logits = br_ref[...]
    for a in range(3):
        for b in range(3 - a):
            logits = logits + jnp.dot(hs[a], ws[b], preferred_element_type=jnp.float32)

    tm = logits.shape[0]
    lane = lax.broadcasted_iota(jnp.int32, (tm, N_EXPERTS), 1).astype(jnp.float32)
    work, vals, firsts, hots = logits, [], [], []
    for _ in range(TOP_K):
        m = jnp.max(work, axis=-1, keepdims=True)
        first = jnp.min(jnp.where(work == m, lane, float(N_EXPERTS)), axis=-1, keepdims=True)
        hot = lane == first
        vals.append(m)
        firsts.append(first)
        hots.append(hot)
        work = jnp.where(hot, -jnp.inf, work)
    exps = [jnp.exp(v - vals[0]) for v in vals]
    den = exps[0] + exps[1] + exps[2] + exps[3]

    onehot = jnp.zeros_like(logits)
    for hot in hots:
        onehot = onehot + jnp.where(hot, 1.0, 0.0)
    row_i = lax.broadcasted_iota(jnp.int32, (tm, tm), 0)
    col_i = lax.broadcasted_iota(jnp.int32, (tm, tm), 1)
    earlier = jnp.where(col_i < row_i, 1.0, 0.0).astype(jnp.bfloat16)
    pos = jnp.dot(earlier, onehot.astype(jnp.bfloat16), preferred_element_type=jnp.float32) + run_ref[...]
    ranks = [jnp.sum(jnp.where(hot, pos, 0.0), axis=-1, keepdims=True) for hot in hots]
    run_ref[...] = run_ref[...] + jnp.sum(onehot, axis=0, keepdims=True)
    cnt_ref[...] = run_ref[...]
    e_ref[...] = jnp.concatenate(firsts, axis=1).astype(jnp.int32)
    r_ref[...] = jnp.concatenate(ranks, axis=1).astype(jnp.int32)
    w_ref[...] = jnp.concatenate([e / den for e in exps], axis=1)


def _outproj(oa, ob, zf, x_all, wpa, wpb, wo, g_ffn, w_router, b_router, tm):
    n = x_all.shape[0]
    row = lambda w: pl.BlockSpec((tm, w), lambda i: (i, 0))
    full = lambda a: pl.BlockSpec(a.shape, lambda i: (0,) * a.ndim)
    tiles = pl.BlockSpec((tm, ROW_TILES, LANES), lambda i: (i, 0, 0))
    tiles_shape = jax.ShapeDtypeStruct((n, ROW_TILES, LANES), jnp.float32)
    return pl.pallas_call(
        _outproj_kernel,
        grid=(n // tm,),
        in_specs=[row(oa.shape[1]), row(ob.shape[1]),
                  pl.BlockSpec((tm, D_MODEL), lambda i: (i, 0)), pl.BlockSpec((tm, D_MODEL), lambda i: (i, 1)),
                  row(D_MODEL), full(wpa), full(wpb), full(wo), full(g_ffn), full(w_router), full(b_router)],
        out_specs=[tiles, tiles, row(TOP_K), row(TOP_K), row(TOP_K),
                   pl.BlockSpec((1, N_EXPERTS), lambda i: (0, 0))],
        out_shape=[tiles_shape, tiles_shape,
                   jax.ShapeDtypeStruct((n, TOP_K), jnp.int32), jax.ShapeDtypeStruct((n, TOP_K), jnp.int32),
                   jax.ShapeDtypeStruct((n, TOP_K), jnp.float32),
                   jax.ShapeDtypeStruct((1, N_EXPERTS), jnp.float32)],
        scratch_shapes=[pltpu.VMEM((1, N_EXPERTS), jnp.float32)],
        compiler_params=_cparams("arbitrary"),
        name="outproj_router",
    )(oa, ob, zf, zf, x_all, wpa, wpb, wo, g_ffn, w_router, b_router)


MOE_BLOCK_ROWS = 256
MOE_DMA_TOKENS = 256


def _row_copy(src, dst, sem):
    return pltpu.make_async_copy(src, dst, sem)


def _dispatch_kernel(dest_ref, h_ref, xs_in_ref, xs_ref, sem):
    del xs_in_ref
    n_tok = h_ref.shape[0]

    def issue(t, _):
        for s in range(TOP_K):
            _row_copy(h_ref.at[t], xs_ref.at[dest_ref[0, 0, t * TOP_K + s]], sem).start()
        return 0

    lax.fori_loop(0, n_tok, issue, 0)

    def drain(t, _):
        for s in range(TOP_K):
            _row_copy(h_ref.at[t], xs_ref.at[dest_ref[0, 0, t * TOP_K + s]], sem).wait()
        return 0

    lax.fori_loop(0, n_tok, drain, 0)


def _dispatch(dest, h2_tiles, n_rows, td):
    n = h2_tiles.shape[0]
    zeros = jnp.zeros((n_rows, ROW_TILES, LANES), jnp.float32)
    return pl.pallas_call(
        _dispatch_kernel,
        grid=(n // td,),
        in_specs=[pl.BlockSpec((1, 1, td * TOP_K), lambda i: (i, 0, 0), memory_space=pltpu.SMEM),
                  pl.BlockSpec((td, ROW_TILES, LANES), lambda i: (i, 0, 0)),
                  pl.BlockSpec(memory_space=pl.ANY)],
        out_specs=pl.BlockSpec(memory_space=pl.ANY),
        out_shape=jax.ShapeDtypeStruct(zeros.shape, jnp.float32),
        scratch_shapes=[pltpu.SemaphoreType.DMA(())],
        input_output_aliases={2: 0},
        compiler_params=_cparams("arbitrary"),
        name="moe_dispatch",
    )(dest.reshape(n // td, 1, td * TOP_K), h2_tiles, zeros)


def _ffn_kernel(blk_e_ref, n_used_ref, xs_ref, wug_ref, bug_ref, wd_ref, bd_ref, ys_ref, wug_scr, wd_scr):
    i = pl.program_id(0)
    e = blk_e_ref[i]
    e_prev = blk_e_ref[jnp.maximum(i - 1, 0)]

    @pl.when((i == 0) | (e != e_prev))
    def _():
        wug_scr[...] = wug_ref[0].astype(jnp.bfloat16)
        wd_scr[...] = wd_ref[0].astype(jnp.bfloat16)

    @pl.when(i < n_used_ref[0])
    def _():
        x = _tiles_to_rows(xs_ref).astype(jnp.bfloat16)
        ug = jnp.dot(x, wug_scr[...], preferred_element_type=jnp.float32) + bug_ref[0]
        gt = jnp.minimum(ug[:, :D_FF], SWIGLU_LIMIT)
        up = jnp.clip(ug[:, D_FF:], -SWIGLU_LIMIT, SWIGLU_LIMIT)
        act = (up + 1.0) * gt * jax.nn.sigmoid(gt * SWIGLU_ALPHA)
        out = jnp.dot(act.astype(jnp.bfloat16), wd_scr[...], preferred_element_type=jnp.float32) + bd_ref[0]
        _rows_to_tiles(ys_ref, out)

    @pl.when(i >= n_used_ref[0])
    def _():
        ys_ref[...] = jnp.zeros(ys_ref.shape, jnp.float32)


def _ffn(blk_e, n_used, xs, w_ug, b_ug, w_down, b_down):
    n_blocks = xs.shape[0] // MOE_BLOCK_ROWS
    rows = pl.BlockSpec((MOE_BLOCK_ROWS, ROW_TILES, LANES), lambda i, be, nu: (i, 0, 0))
    exp = lambda a: pl.BlockSpec((1,) + a.shape[1:], lambda i, be, nu: (be[i], 0, 0))
    return pl.pallas_call(
        _ffn_kernel,
        grid_spec=pltpu.PrefetchScalarGridSpec(
            num_scalar_prefetch=2, grid=(n_blocks,),
            in_specs=[rows, exp(w_ug), exp(b_ug), exp(w_down), exp(b_down)],
            out_specs=rows,
            scratch_shapes=[pltpu.VMEM(w_ug.shape[1:], jnp.bfloat16), pltpu.VMEM(w_down.shape[1:], jnp.bfloat16)]),
        out_shape=jax.ShapeDtypeStruct(xs.shape, jnp.float32),
        compiler_params=_cparams("arbitrary"),
        name="moe_ffn",
    )(blk_e, n_used, xs, w_ug, b_ug, w_down, b_down)


def _combine_kernel(dest_ref, w_ref, x1_ref, ys_ref, y_ref, buf, sem):
    n_tok = x1_ref.shape[0]

    def issue(t, _):
        for s in range(TOP_K):
            _row_copy(ys_ref.at[dest_ref[0, 0, t * TOP_K + s]], buf.at[s, t], sem).start()
        return 0

    lax.fori_loop(0, n_tok, issue, 0)

    def drain(t, _):
        for s in range(TOP_K):
            _row_copy(ys_ref.at[dest_ref[0, 0, t * TOP_K + s]], buf.at[s, t], sem).wait()
        return 0

    lax.fori_loop(0, n_tok, drain, 0)

    def token(t, _):
        acc = x1_ref[t]
        for s in range(TOP_K):
            acc = acc + w_ref[0, 0, t * TOP_K + s] * buf[s, t]
        y_ref[t] = acc
        return 0

    lax.fori_loop(0, n_tok, token, 0)


def _combine(dest, weight, x1_tiles, ys, td):
    n = x1_tiles.shape[0]
    smem = pl.BlockSpec((1, 1, td * TOP_K), lambda i: (i, 0, 0), memory_space=pltpu.SMEM)
    tiles = pl.BlockSpec((td, ROW_TILES, LANES), lambda i: (i, 0, 0))
    return pl.pallas_call(
        _combine_kernel,
        grid=(n // td,),
        in_specs=[smem, smem, tiles, pl.BlockSpec(memory_space=pl.ANY)],
        out_specs=tiles,
        out_shape=jax.ShapeDtypeStruct(x1_tiles.shape, jnp.float32),
        scratch_shapes=[pltpu.VMEM((TOP_K, td, ROW_TILES, LANES), jnp.float32), pltpu.SemaphoreType.DMA(())],
        compiler_params=_cparams("arbitrary"),
        name="moe_combine",
    )(dest.reshape(n // td, 1, td * TOP_K), weight.reshape(n // td, 1, td * TOP_K), x1_tiles, ys)


def _final_norm_kernel(x_ref, g_ref, y_ref):
    x = _tiles_to_rows(x_ref)
    ms = jnp.mean(x * x, axis=-1, keepdims=True)
    y_ref[...] = x * lax.rsqrt(ms + RMS_EPS) * g_ref[...]


def _final_norm(x_tiles, g_final, tm):
    n = x_tiles.shape[0]
    return pl.pallas_call(
        _final_norm_kernel,
        grid=(n // tm,),
        in_specs=[pl.BlockSpec((tm, ROW_TILES, LANES), lambda i: (i, 0, 0)),
                  pl.BlockSpec((1, D_MODEL), lambda i: (0, 0))],
        out_specs=pl.BlockSpec((tm, D_MODEL), lambda i: (i, 0)),
        out_shape=jax.ShapeDtypeStruct((n, D_MODEL), jnp.float32),
        compiler_params=_cparams("parallel"),
        name="final_norm",
    )(x_tiles, g_final)


def _moe(x1_tiles, h2_tiles, top_e, rank, weight, counts, w_ug, b_ug, w_down, b_down, g_final, tm):
    n = x1_tiles.shape[0]
    blk = MOE_BLOCK_ROWS
    n_blocks = -(-(n * TOP_K + N_EXPERTS * (blk - 1)) // blk)
    counts = counts.reshape(N_EXPERTS).astype(jnp.int32)
    padded = (counts + blk - 1) // blk * blk
    ends = jnp.cumsum(padded)
    dest = ((ends - padded)[top_e] + rank).astype(jnp.int32).reshape(n * TOP_K)
    blk_e = jnp.minimum(jnp.searchsorted(ends, jnp.arange(n_blocks, dtype=jnp.int32) * blk, side="right"),
                        N_EXPERTS - 1).astype(jnp.int32)
    n_used = (ends[-1:] // blk).astype(jnp.int32)
    td = MOE_DMA_TOKENS if n % MOE_DMA_TOKENS == 0 else tm
    xs = _dispatch(dest, h2_tiles, n_blocks * blk, td)
    ys = _ffn(blk_e, n_used, xs, w_ug, b_ug, w_down, b_down)
    y_tiles = _combine(dest, weight.reshape(n * TOP_K), x1_tiles, ys, td)
    return _final_norm(y_tiles, g_final, tm)


def _zf_col(name):
    return (ZF_GATES + ZB[name][0]) // ZB[name][1]


def _zb_col(name):
    return ZB[name][0] // ZB[name][1]


def _layer(x_p, x_s, caches, page_table, w, *, tm=256, tq_prompt=128, tk_prompt=512, pps=8):
    (c_ak, c_av, c_aik, c_bck, c_bcv, c_bsk, c_bsv, s_wk, s_wv) = caches
    bp, s_len, _ = x_p.shape
    bs, t_new, _ = x_s.shape
    n_p, n_s_tok = bp * s_len, bs * t_new
    past = page_table.shape[1] * PAGE_SIZE
    l_s = past + t_new

    x_all = jnp.concatenate([x_p.reshape(n_p, D_MODEL), x_s.reshape(n_s_tok, D_MODEL)], axis=0)
    pos_table = jnp.concatenate([jnp.arange(s_len), jnp.tile(past + jnp.arange(t_new), bs)])
    zf, zb = _inproj(x_all, w["g_mix"].reshape(1, D_MODEL), _permute_w_in(w["w_in"]), _rope_table(pos_table),
                     s_len // tm, n_p // tm, tm)
    wk = _per_head_weights(w["w_cmp_k"])
    wv = _per_head_weights(w["w_cmp_v"])

    zf_p = zf[:n_p].reshape(bp, s_len, ZF_W)
    zb_p = zb[:n_p].reshape(bp, s_len, ZB_W)
    n_c_p = (s_len - CMP_LEN) // CMP_STRIDE + 1
    kc_p, vc_p = _compress((zf_p, _zf_col("kcb")), (zf_p, _zf_col("vcb")), wk, wv,
                           batch=bp, rows=s_len, nb=n_c_p + 1)
    oa_p = _dsa((zb_p, _zb_col("qa")), (zb_p, _zb_col("qi")), (zf_p, _zf_col("misc")),
                (zb_p, _zb_col("misc")), (zb_p, _zb_col("ka")), (zb_p, _zb_col("va")),
                batch=bp, t_q=s_len, tq=tq_prompt, l_pad=s_len, pos_base=0, k_top=min(IDX_TOPK_MAX, s_len // 4),
                fm=False, tk=tk_prompt)
    ob_p = _nsa((zb_p, _zb_col("qb")), (zf_p, _zf_col("misc")), kc_p, vc_p,
                (zb_p, _zb_col("ksb")), (zb_p, _zb_col("vsb")), (zb_p, _zb_col("kwb")), (zb_p, _zb_col("vwb")),
                batch=bp, t_q=s_len, tq=tq_prompt, l_pad=s_len, w_pad=s_len, pos_base=0,
                n_c=n_c_p, n_s=-(-s_len // SEL_BLOCK), win_base=0, fm=False, tk=tk_prompt)

    zf_s = zf[n_p:].reshape(bs, t_new, ZF_W)
    new = lambda name: zf_s[:, :, ZF_GATES + ZB[name][0]:ZF_GATES + ZB[name][0] + ZB[name][1]]
    bf, f32 = jnp.bfloat16, jnp.float32
    span = pps * PAGE_SIZE
    tk_s = span

    def fm_page(c):
        return jnp.transpose(c, (0, 2, 3, 1)).reshape(c.shape[0], KV_W, PAGE_SIZE)

    def fm_rows(a, pad_to):
        a = jnp.transpose(a, (0, 2, 1))
        return jnp.pad(a, ((0, 0), (0, 0), (0, pad_to - a.shape[-1])))

    row_tail = lambda name: jnp.pad(new(name), ((0, 0), (0, span - t_new), (0, 0)))
    (ak_f, av_f, aik_f, bck_f, bcv_f, bsk_f, bsv_f) = _gather_pages(
        page_table,
        [fm_page(c_ak), fm_page(c_av), jnp.transpose(c_aik, (0, 2, 1)), fm_page(c_bck), fm_page(c_bcv),
         fm_page(c_bsk), fm_page(c_bsv)],
        [fm_rows(new("ka"), span), fm_rows(new("va"), span), fm_rows(new("misc")[:, :, :IDX_DIM], span),
         row_tail("kcb"), row_tail("vcb"), fm_rows(new("ksb"), span), fm_rows(new("vsb"), span)],
        [bf, bf, bf, f32, f32, bf, bf],
        [False, False, False, True, True, False, False], pps)
    l_pad = past + span
    n_c_s = (l_s - CMP_LEN) // CMP_STRIDE + 1
    kc_s, vc_s = _compress((bck_f, 0), (bcv_f, 0), wk, wv, batch=bs, rows=l_pad, nb=n_c_s + 1)
    w_buf = s_wk.shape[1]
    w_pad = -(-(w_buf + t_new) // tk_s) * tk_s
    win = lambda state, name: jnp.concatenate(
        [jnp.transpose(state, (0, 2, 3, 1)).reshape(bs, KV_W, w_buf), fm_rows(new(name), w_pad - w_buf)], axis=-1)
    kw_all, vw_all = win(s_wk, "kwb"), win(s_wv, "vwb")
    oa_s = _dsa((zf_s, _zf_col("qa")), (zf_s, _zf_col("qi")), (zf_s, _zf_col("misc")),
                (aik_f, 0), (ak_f, 0), (av_f, 0), batch=bs, t_q=t_new, tq=t_new, l_pad=l_pad, pos_base=past,
                k_top=min(IDX_TOPK_MAX, l_s // 4), fm=True, tk=tk_s)
    ob_s = _nsa((zf_s, _zf_col("qb")), (zf_s, _zf_col("misc")), kc_s, vc_s, (bsk_f, 0), (bsv_f, 0),
                (kw_all.astype(bf), 0), (vw_all.astype(bf), 0),
                batch=bs, t_q=t_new, tq=t_new, l_pad=l_pad, w_pad=w_pad, pos_base=past,
                n_c=n_c_s, n_s=-(-l_s // SEL_BLOCK), win_base=past - w_buf, fm=True, tk=tk_s)

    oa = jnp.concatenate([oa_p.reshape(n_p, -1), oa_s.reshape(n_s_tok, -1)], axis=0)
    ob = jnp.concatenate([ob_p.reshape(n_p, -1), ob_s.reshape(n_s_tok, -1)], axis=0)
    x1, h2, top_e, rank, weight, counts = _outproj(
        oa, ob, zf, x_all, _pair_head_rows(w["w_proj_a"]).astype(bf), _pair_head_rows(w["w_proj_b"]).astype(bf),
        w["w_out"].astype(bf), w["g_ffn"].reshape(1, D_MODEL), w["w_router"], w["b_router"].reshape(1, N_EXPERTS), tm)
    y = _moe(x1, h2, top_e, rank, weight, counts, w["w_ug"], w["b_ug"].reshape(N_EXPERTS, 1, 2 * D_FF),
             w["w_down"], w["b_down"].reshape(N_EXPERTS, 1, D_MODEL), w["g_final"].reshape(1, D_MODEL), tm)

    heads = lambda a: a.reshape(a.shape[0], a.shape[1], A_KV_HEADS, HEAD_DIM)
    col_p = lambda name: zf_p[:, :, ZF_GATES + ZB[name][0]:ZF_GATES + ZB[name][0] + ZB[name][1]]
    w_keep = min(WINDOW, s_len)
    st_p = (heads(col_p("ka")), heads(col_p("va")), col_p("misc")[:, :, :IDX_DIM], heads(col_p("kcb")),
            heads(col_p("vcb")), heads(col_p("ksb")), heads(col_p("vsb")),
            heads(col_p("kwb")[:, s_len - w_keep:]), heads(col_p("vwb")[:, s_len - w_keep:]))
    st_s = (heads(new("ka")), heads(new("va")), new("misc")[:, :, :IDX_DIM], heads(new("kcb")), heads(new("vcb")),
            heads(new("ksb")), heads(new("vsb")),
            heads(jnp.transpose(kw_all[..., t_new:t_new + w_buf], (0, 2, 1))),
            heads(jnp.transpose(vw_all[..., t_new:t_new + w_buf], (0, 2, 1))))
    return y[:n_p].reshape(bp, s_len, D_MODEL), y[n_p:].reshape(bs, t_new, D_MODEL), st_p, st_s


def kernel(x_prompt, x_sample, cache_a_k, cache_a_v, cache_a_idx_k, cache_b_cmp_k, cache_b_cmp_v, cache_b_slc_k,
           cache_b_slc_v, state_b_win_k, state_b_win_v, page_table, g_mix, w_in, w_cmp_k, w_cmp_v, w_proj_a,
           w_proj_b, w_out, g_ffn, w_router, b_router, w_ug, b_ug, w_down, b_down, g_final):
    depth = g_mix.shape[0]
    assert depth == 1, "the final rmsnorm is fused into the last layer's expert kernel; one layer is supported"
    caches = (cache_a_k[0], cache_a_v[0], cache_a_idx_k[0], cache_b_cmp_k[0], cache_b_cmp_v[0], cache_b_slc_k[0],
              cache_b_slc_v[0], state_b_win_k[0], state_b_win_v[0])
    w = dict(g_mix=g_mix[0], w_in=w_in[0], w_cmp_k=w_cmp_k[0], w_cmp_v=w_cmp_v[0], w_proj_a=w_proj_a[0],
             w_proj_b=w_proj_b[0], w_out=w_out[0], g_ffn=g_ffn[0], w_router=w_router[0], b_router=b_router[0],
             w_ug=w_ug[0], b_ug=b_ug[0], w_down=w_down[0], b_down=b_down[0], g_final=g_final)
    y_p, y_s, st_p, st_s = _layer(x_prompt, x_sample, caches, page_table, w)
    return (y_p, y_s) + tuple(s[None] for s in st_p) + tuple(s[None] for s in st_s)
```
